```python
import math
import jax
import jax.numpy as jnp
from jax import lax
import numpy as np

D_MODEL = 1024
BATCH = 2
SEQ = 8192
DEPTH = 4
DEC_BATCH = 128
DEC_SEQ = 8
PAST_LEN = 2048
PAGE_SIZE = 128

HEAD_DIM = 64
GROUP_WIDTH = D_MODEL // 4
N_HEADS_G = GROUP_WIDTH // HEAD_DIM
HG_DK = HEAD_DIM
HG_DV = HEAD_DIM
HG_CHUNK = 64
MOBA_BLOCK = 256
MOBA_TOPK = 3
ROT_DIM = HEAD_DIM // 4
ROPE_THETA = 500000.0
Q_BLOCK = 128
POOL_WINDOWS = (2, 4, 8, 16)
POOL_GC = GROUP_WIDTH // len(POOL_WINDOWS)
POOL_BUF = max(POOL_WINDOWS) - 1
N_GROUPS = 4
EXP_PER_GROUP = 4
N_EXPERTS = N_GROUPS * EXP_PER_GROUP
TOPK_IN_GROUP = 2
D_EXPERT = D_MODEL // 2
IN_COLS = 11 * GROUP_WIDTH
EPS = 1e-6
NEG_BIG = -1e30
F_FLOOR = 1e-30

kernel_name = 'hybrid_hgrn2_moba_stickbreak_pool_hmoe_step'


def _rmsnorm(x, g):
    xf = x.astype(jnp.float32)
    y = xf * lax.rsqrt(jnp.mean(xf * xf, axis=-1, keepdims=True) + EPS)
    return (y * g.astype(jnp.float32)).astype(x.dtype)


def _qblock(T):
    return Q_BLOCK if T % Q_BLOCK == 0 else T


def _heads(z):
    B, T, _ = z.shape
    return z.reshape(B, T, N_HEADS_G, HEAD_DIM).transpose(0, 2, 1, 3)


def _merge(z):
    B, H, T, d = z.shape
    return z.transpose(0, 2, 1, 3).reshape(B, T, H * d)


def _rope(x, pos):
    half = ROT_DIM // 2
    inv = jnp.exp(-math.log(ROPE_THETA) * jnp.arange(half, dtype=jnp.float32) * 2.0 / ROT_DIM)
    ang = pos.astype(jnp.float32)[:, None] * inv[None, :]
    cos, sin = jnp.cos(ang), jnp.sin(ang)
    xf = x.astype(jnp.float32)
    x1, x2 = xf[..., :half], xf[..., half:ROT_DIM]
    out = jnp.concatenate([x1 * cos - x2 * sin, x2 * cos + x1 * sin, xf[..., ROT_DIM:]], axis=-1)
    return out.astype(x.dtype)


def _hgrn_lower_bounds(lb_logits):
    p = jax.nn.softmax(lb_logits.astype(jnp.float32), axis=0)
    return jnp.cumsum(p, axis=0) - p[0:1]


def _hgrn2(q, f_pre, inp, lb, S0):
    B, H, T, dk = q.shape
    dv = inp.shape[-1]
    C = HG_CHUNK if T % HG_CHUNK == 0 else T
    n = T // C
    a = f_pre.astype(jnp.float32)
    lbb = jnp.clip(lb, 0.0, 1.0)[None, :, None, :]
    f = lbb + (1.0 - lbb) * jax.nn.sigmoid(a)
    log_f = jnp.log(jnp.maximum(f, F_FLOOR))
    kk = (1.0 - lbb) * jax.nn.sigmoid(-a)

    def chunks(z):
        return z.reshape(B, H, n, C, z.shape[-1]).transpose(2, 0, 1, 3, 4)

    mask = jnp.tril(jnp.ones((C, C), dtype=bool))[:, :, None]

    def step(S, xs):
        qc, lfc, kc, ic = xs
        b = jnp.cumsum(lfc, axis=2)
        o = jnp.einsum('bhtk,bhkv->bhtv', qc * jnp.exp(b), S)
        diff = b[:, :, :, None, :] - b[:, :, None, :, :]
        dec = jnp.where(mask, jnp.exp(jnp.where(mask, diff, 0.0)), 0.0)
        att = jnp.einsum('bhtk,bhtsk,bhsk->bhts', qc, dec, kc)
        o = o + jnp.einsum('bhts,bhsv->bhtv', att, ic)
        bl = b[:, :, -1, :]
        S = jnp.exp(bl)[..., None] * S + jnp.einsum('bhsk,bhsv->bhkv', kc * jnp.exp(bl[:, :, None, :] - b), ic)
        return S, o

    S, o = lax.scan(step, S0.astype(jnp.float32),
                    (chunks(q.astype(jnp.float32)), chunks(log_f), chunks(kk), chunks(inp.astype(jnp.float32))))
    o = o.transpose(1, 2, 0, 3, 4).reshape(B, H, T, dv)
    return o, S


def _moba(q, k, v, pos0):
    B, H, T, hd = q.shape
    L = k.shape[2]
    qb = _qblock(T)
    nqb = T // qb
    nb = -(-L // MOBA_BLOCK)
    nsel = min(MOBA_TOPK, nb)
    pad = (nb + 1) * MOBA_BLOCK + qb - L
    kp = jnp.pad(k, ((0, 0), (0, 0), (0, pad), (0, 0)))
    vp = jnp.pad(v, ((0, 0), (0, 0), (0, pad), (0, 0)))
    kblk = kp[:, :, :nb * MOBA_BLOCK].reshape(B, H, nb, MOBA_BLOCK, hd)
    vblk = vp[:, :, :nb * MOBA_BLOCK].reshape(B, H, nb, MOBA_BLOCK, hd)
    kmean = jnp.mean(kblk.astype(jnp.float32), axis=3)
    scale = 1.0 / math.sqrt(hd)
    bi = jnp.arange(B)[:, None, None, None]
    hi = jnp.arange(H)[None, :, None, None]
    qs = q.reshape(B, H, nqb, qb, hd).transpose(2, 0, 1, 3, 4)

    def one(args):
        qc, j = args
        p0 = pos0 + j * qb
        tpos = p0 + jnp.arange(qb)
        own = tpos // MOBA_BLOCK
        s = jnp.einsum('bhqd,bhnd->bhqn', qc.astype(jnp.float32), kmean)
        s = jnp.where(jnp.arange(nb)[None, :] < own[:, None], s, NEG_BIG)
        _, idx = lax.top_k(s, nsel)
        valid = jnp.arange(nsel)[None, :] < own[:, None]
        ks = kblk[bi, hi, idx]
        vs = vblk[bi, hi, idx]
        ls = jnp.einsum('bhqd,bhqnjd->bhqnj', qc, ks).astype(jnp.float32) * scale
        ls = jnp.where(valid[:, :, None], ls, NEG_BIG).reshape(B, H, qb, nsel * MOBA_BLOCK)
        start = (p0 // MOBA_BLOCK) * MOBA_BLOCK
        kw = lax.dynamic_slice_in_dim(kp, start, MOBA_BLOCK + qb, axis=2)
        vw = lax.dynamic_slice_in_dim(vp, start, MOBA_BLOCK + qb, axis=2)
        kpos = start + jnp.arange(MOBA_BLOCK + qb)
        own_mask = ((kpos[None, :] // MOBA_BLOCK) == own[:, None]) & (kpos[None, :] <= tpos[:, None])
        lo = jnp.einsum('bhqd,bhkd->bhqk', qc, kw).astype(jnp.float32) * scale
        lo = jnp.where(own_mask, lo, NEG_BIG)
        w = jax.nn.softmax(jnp.concatenate([ls, lo], axis=-1), axis=-1)
        ws = w[..., :nsel * MOBA_BLOCK].reshape(B, H, qb, nsel, MOBA_BLOCK).astype(v.dtype)
        wo = w[..., nsel * MOBA_BLOCK:].astype(v.dtype)
        return jnp.einsum('bhqnj,bhqnjd->bhqd', ws, vs) + jnp.einsum('bhqk,bhkd->bhqd', wo, vw)

    o = lax.map(one, (qs, jnp.arange(nqb)))
    return o.transpose(1, 2, 0, 3, 4).reshape(B, H, T, hd)


def _stick_breaking(q, k, v, pos0):
    B, H, T, hd = q.shape
    L = k.shape[2]
    qb = _qblock(T)
    nqb = T // qb
    kpos = jnp.arange(L)
    scale = 1.0 / math.sqrt(hd)
    qs = q.reshape(B, H, nqb, qb, hd).transpose(2, 0, 1, 3, 4)

    def one(args):
        qc, j = args
        tpos = pos0 + j * qb + jnp.arange(qb)
        z = jnp.einsum('bhqd,bhkd->bhqk', qc, k).astype(jnp.float32) * scale
        m = kpos[None, :] < tpos[:, None]
        lneg = jnp.where(m, jax.nn.log_sigmoid(-z), 0.0)
        rest = jnp.minimum(lax.cumsum(lneg, axis=3, reverse=True) - lneg, 0.0)
        a = jnp.where(m, jnp.exp(jax.nn.log_sigmoid(z) + rest), 0.0)
        return jnp.einsum('bhqk,bhkd->bhqd', a.astype(v.dtype), v)

    o = lax.map(one, (qs, jnp.arange(nqb)))
    return o.transpose(1, 2, 0, 3, 4).reshape(B, H, T, hd)


def _pool_mixer(u, buf, pos0, pool_w_l, pool_scale_l):
    B, T, C = u.shape
    xp = jnp.concatenate([buf.astype(u.dtype), u], axis=1)
    cs = jnp.cumsum(xp.astype(jnp.float32), axis=1)
    cs = jnp.concatenate([jnp.zeros((B, 1, C), jnp.float32), cs], axis=1)
    pos = (pos0 + jnp.arange(T)).astype(jnp.float32)
    hi = cs[:, POOL_BUF + 1:POOL_BUF + 1 + T]
    uf = u.astype(jnp.float32)
    outs = []
    for gi, w in enumerate(POOL_WINDOWS):
        sl = slice(gi * POOL_GC, (gi + 1) * POOL_GC)
        lo = cs[:, POOL_BUF + 1 - w:POOL_BUF + 1 - w + T, sl]
        cnt = jnp.minimum(float(w), pos + 1.0)[None, :, None]
        d = (hi[..., sl] - lo) / cnt - uf[..., sl]
        outs.append(jnp.einsum('btc,cd->btd', d, pool_w_l[gi].astype(jnp.float32)))
    out = jnp.concatenate(outs, axis=-1) * pool_scale_l.astype(jnp.float32)
    return out.astype(u.dtype), xp[:, -POOL_BUF:]


def _cat(past, new):
    return new if past is None else jnp.concatenate([past.astype(new.dtype), new], axis=2)


def _token_mixers(h, pos0, past, S0, pbuf, w_in_l, lb_l, hg_g, pool_w_l, pool_scale_l, w_out_l):
    B, T, _ = h.shape
    u = h @ w_in_l
    qa, fa, ia, ga, qb_, kb_, vb_, qc_, kc_, vc_, ud = jnp.split(u, 11, axis=-1)
    pos = pos0 + jnp.arange(T)
    oa, S_new = _hgrn2(_heads(qa), _heads(fa), _heads(ia), lb_l.reshape(N_HEADS_G, HG_DK), S0)
    oa = oa * lax.rsqrt(jnp.mean(oa * oa, axis=-1, keepdims=True) + EPS)
    oa = oa * hg_g.astype(jnp.float32).reshape(N_HEADS_G, HG_DV)[None, :, None, :]
    oa = _merge(oa).astype(h.dtype) * jax.nn.silu(ga)
    kB = _rope(_heads(kb_), pos)
    vB = _heads(vb_)
    pkb, pvb, pkc, pvc = (None, None, None, None) if past is None else past
    ob = _moba(_rope(_heads(qb_), pos), _cat(pkb, kB), _cat(pvb, vB), pos0)
    kC = _heads(kc_)
    vC = _heads(vc_)
    oc = _stick_breaking(_heads(qc_), _cat(pkc, kC), _cat(pvc, vC), pos0)
    od, buf_new = _pool_mixer(ud, pbuf, pos0, pool_w_l, pool_scale_l)
    merged = jnp.concatenate([oa, _merge(ob), _merge(oc), od], axis=-1) @ w_out_l
    rows = lambda z: z.transpose(0, 2, 1, 3)
    return merged, (rows(kB), rows(vB), rows(kC), rows(vC), S_new, buf_new)


def _moe(h, rgw, rgb, rew, reb, ewg, ewu, ewd):
    B, T, D = h.shape
    hf = h.reshape(B * T, D)
    pg = jax.nn.softmax((hf @ rgw + rgb).astype(jnp.float32), axis=-1)
    g_top, g_idx = lax.top_k(pg, 1)
    el = (hf @ rew + reb).astype(jnp.float32).reshape(-1, N_GROUPS, EXP_PER_GROUP)
    el_g = jnp.take_along_axis(el, g_idx[:, :, None], axis=1)[:, 0, :]
    pe = jax.nn.softmax(el_g, axis=-1)
    e_top, e_idx = lax.top_k(pe, TOPK_IN_GROUP)
    wk = g_top * e_top / jnp.sum(e_top, axis=-1, keepdims=True)
    eid = g_idx * EXP_PER_GROUP + e_idx
    gate = jnp.sum(jax.nn.one_hot(eid, N_EXPERTS, dtype=jnp.float32) * wk[..., None], axis=1)
    y = jnp.zeros_like(hf)
    for e in range(N_EXPERTS):
        a = jax.nn.silu(hf @ ewg[e]) * (hf @ ewu[e])
        y = y + gate[:, e:e + 1].astype(hf.dtype) * (a @ ewd[e])
    return y.reshape(B, T, D)


def _block(x, c, pos0, past, S0, pbuf, ada_w_l, ada_b_l, g_mix, g_ffn, w_in_l, lb_l, hg_g,
           pool_w_l, pool_scale_l, w_out_l, rgw, rgb, rew, reb, ewg, ewu, ewd):
    mod = (c @ ada_w_l + ada_b_l)[:, None, :]
    sh1, sc1, gt1, sh2, sc2, gt2 = jnp.split(mod, 6, axis=-1)
    h = _rmsnorm(x, g_mix) * (1.0 + sc1) + sh1
    mix, st = _token_mixers(h, pos0, past, S0, pbuf, w_in_l, lb_l, hg_g, pool_w_l, pool_scale_l, w_out_l)
    x = x + gt1 * mix
    h = _rmsnorm(x, g_ffn) * (1.0 + sc2) + sh2
    x = x + gt2 * _moe(h, rgw, rgb, rew, reb, ewg, ewu, ewd)
    return x, st


def _gather_past(cache, l, page_table):
    g = cache[l, page_table]
    DB, NP, PS, H, hd = g.shape
    return g.reshape(DB, NP * PS, H, hd).transpose(0, 2, 1, 3)


def setup_inputs(seed: int = 0) -> dict:
    key = jax.random.key(seed)
    ks = jax.random.split(key, 32)
    n_pages = PAST_LEN // PAGE_SIZE
    n_used = DEC_BATCH * n_pages
    n_phys = (n_used * 5) // 4
    D, H, hd, GW = D_MODEL, N_HEADS_G, HEAD_DIM, GROUP_WIDTH

    def nrm(k, shape, s):
        return s * jax.random.normal(k, shape, jnp.float32)

    kv_shape = (DEPTH, n_phys, PAGE_SIZE, H, hd)
    page_table = jax.random.permutation(ks[0], n_phys)[:n_used].reshape(DEC_BATCH, n_pages).astype(jnp.int32)
    return dict(
        x_prompt=nrm(ks[1], (BATCH, SEQ, D), 1.0),
        x_sample=nrm(ks[2], (DEC_BATCH, DEC_SEQ, D), 1.0),
        cache_moba_k=nrm(ks[3], kv_shape, 1.0),
        cache_moba_v=nrm(ks[4], kv_shape, 1.0),
        cache_sb_k=nrm(ks[5], kv_shape, 1.0),
        cache_sb_v=nrm(ks[6], kv_shape, 1.0),
        state_hgrn=nrm(ks[7], (DEPTH, DEC_BATCH, H, HG_DK, HG_DV), 0.5),
        state_pool=nrm(ks[8], (DEPTH, DEC_BATCH, POOL_BUF, GW), 1.0),
        page_table=page_table,
        c_prompt=nrm(ks[9], (BATCH, D), 1.0),
        c_sample=nrm(ks[10], (DEC_BATCH, D), 1.0),
        ada_w=nrm(ks[11], (DEPTH, D, 6 * D), 0.5 * D ** -0.5),
        ada_b=nrm(ks[12], (DEPTH, 6 * D), 0.01),
        norm_mix_g=1.0 + nrm(ks[13], (DEPTH, D), 0.05),
        norm_ffn_g=1.0 + nrm(ks[14], (DEPTH, D), 0.05),
        w_in=nrm(ks[15], (DEPTH, D, IN_COLS), D ** -0.5),
        hgrn_lb=nrm(ks[16], (DEPTH, GW), 1.0),
        hgrn_norm_g=1.0 + nrm(ks[17], (DEPTH, GW), 0.05),
        pool_w=nrm(ks[18], (DEPTH, len(POOL_WINDOWS), POOL_GC, POOL_GC), POOL_GC ** -0.5),
        pool_scale=1.0 + nrm(ks[19], (DEPTH, GW), 0.05),
        w_out=nrm(ks[20], (DEPTH, 4 * GW, D), (4 * GW) ** -0.5),
        router_group_w=nrm(ks[21], (DEPTH, D, N_GROUPS), D ** -0.5),
        router_group_b=nrm(ks[22], (DEPTH, N_GROUPS), 0.01),
        router_expert_w=nrm(ks[23], (DEPTH, D, N_EXPERTS), D ** -0.5),
        router_expert_b=nrm(ks[24], (DEPTH, N_EXPERTS), 0.01),
        expert_w_gate=nrm(ks[25], (DEPTH, N_EXPERTS, D, D_EXPERT), D ** -0.5),
        expert_w_up=nrm(ks[26], (DEPTH, N_EXPERTS, D, D_EXPERT), D ** -0.5),
        expert_w_down=nrm(ks[27], (DEPTH, N_EXPERTS, D_EXPERT, D), D_EXPERT ** -0.5),
        final_norm_g=1.0 + nrm(ks[28], (D,), 0.05),
    )


def reference(x_prompt, x_sample, cache_moba_k, cache_moba_v, cache_sb_k, cache_sb_v,
              state_hgrn, state_pool, page_table, c_prompt, c_sample,
              ada_w, ada_b, norm_mix_g, norm_ffn_g, w_in, hgrn_lb, hgrn_norm_g,
              pool_w, pool_scale, w_out, router_group_w, router_group_b,
              router_expert_w, router_expert_b, expert_w_gate, expert_w_up,
              expert_w_down, final_norm_g):
    lbs = _hgrn_lower_bounds(hgrn_lb)
    past_len = page_table.shape[1] * PAGE_SIZE
    bp = x_prompt.shape[0]
    xp, xs = x_prompt, x_sample
    st_p, st_s = [], []
    for l in range(DEPTH):
        wl = (ada_w[l], ada_b[l], norm_mix_g[l], norm_ffn_g[l], w_in[l], lbs[l], hgrn_norm_g[l],
              pool_w[l], pool_scale[l], w_out[l], router_group_w[l], router_group_b[l],
              router_expert_w[l], router_expert_b[l], expert_w_gate[l], expert_w_up[l], expert_w_down[l])
        s0 = jnp.zeros((bp, N_HEADS_G, HG_DK, HG_DV), jnp.float32)
        buf0 = jnp.zeros((bp, POOL_BUF, GROUP_WIDTH), x_prompt.dtype)
        xp, sp = _block(xp, c_prompt, 0, None, s0, buf0, *wl)
        past = (_gather_past(cache_moba_k, l, page_table), _gather_past(cache_moba_v, l, page_table),
                _gather_past(cache_sb_k, l, page_table), _gather_past(cache_sb_v, l, page_table))
        xs, ss = _block(xs, c_sample, past_len, past, state_hgrn[l], state_pool[l], *wl)
        st_p.append(sp)
        st_s.append(ss)

    def stk(sts, i):
        return jnp.stack([s[i] for s in sts], axis=0)

    y_prompt = _rmsnorm(xp, final_norm_g)
    y_sample = _rmsnorm(xs, final_norm_g)
    return (y_prompt, y_sample,
            stk(st_p, 0), stk(st_p, 1), stk(st_p, 2), stk(st_p, 3), stk(st_p, 4), stk(st_p, 5),
            stk(st_s, 0), stk(st_s, 1), stk(st_s, 2), stk(st_s, 3), stk(st_s, 4), stk(st_s, 5))
```

```python
import functools
import math

import numpy as np
import jax
import jax.numpy as jnp
from jax import lax
from jax.experimental import pallas as pl
from jax.experimental.pallas import tpu as pltpu

F32 = jnp.float32
BF = jnp.bfloat16

D_MODEL = 1024
DEPTH = 4
PAGE_SIZE = 128
HEAD_DIM = 64
GW = 256
NH = 4
HG_CHUNK = 64
MOBA_BLOCK = 256
MOBA_TOPK = 3
ROT_DIM = 16
ROPE_THETA = 500000.0
Q_BLOCK = 128
POOL_WINDOWS = (2, 4, 8, 16)
POOL_BUF = 15
N_GROUPS = 4
EXP_PER_GROUP = 4
N_EXPERTS = 16
D_EXPERT = 512
IN_COLS = 11 * GW
EPS = 1e-6
NEG_BIG = -1e30
F_FLOOR = 1e-30
SCALE = 1.0 / math.sqrt(HEAD_DIM)

V7X_VMEM_BYTES = 64 * 1024 * 1024
VMEM_LIMIT = V7X_VMEM_BYTES - 12 * 1024 * 1024
LANES = 128
ROW_TILE = 512


def _params():
    return pltpu.CompilerParams(vmem_limit_bytes=VMEM_LIMIT)


def _mm(a, b):
    return jnp.dot(a, b, preferred_element_type=F32)


def _mm_nt(a, b):
    return lax.dot_general(a, b, (((1,), (1,)), ((), ())), preferred_element_type=F32)


def _mm_tn(a, b):
    return lax.dot_general(a, b, (((0,), (0,)), ((), ())), preferred_element_type=F32)


def _split2(x):
    hi = x.astype(BF)
    lo = (x - hi.astype(F32)).astype(BF)
    return hi, lo


def _split3(x):
    hi = x.astype(BF)
    r = x - hi.astype(F32)
    mid = r.astype(BF)
    lo = (r - mid.astype(F32)).astype(BF)
    return hi, mid, lo


def _mm_exact_lhs(w01, x):
    hi, mid, lo = _split3(x)
    return _mm(w01, hi) + _mm(w01, mid) + _mm(w01, lo)


def _mm3_nt(a, b):
    ah, al = _split2(a)
    bh, bl = _split2(b)
    return _mm_nt(ah, bh) + _mm_nt(ah, bl) + _mm_nt(al, bh)


def _mm3(a, b):
    ah, al = _split2(a)
    bh, bl = _split2(b)
    return _mm(ah, bh) + _mm(ah, bl) + _mm(al, bh)


def _sigmoid(x):
    return 1.0 / (1.0 + jnp.exp(-x))


def _silu(x):
    return x * _sigmoid(x)


def _log_sigmoid_pair(z):
    t = jnp.log(1.0 + jnp.exp(-jnp.abs(z)))
    return jnp.minimum(z, 0.0) - t, -jnp.maximum(z, 0.0) - t


def _ada_kernel(c_ref, w_ref, b_ref, o_ref):
    o_ref[0] = _mm(c_ref[...].astype(BF), w_ref[0].astype(BF)) + b_ref[0]


def _ada_all(c_all, ada_w, ada_b):
    m = c_all.shape[0]
    tn = 1536
    return pl.pallas_call(
        _ada_kernel,
        out_shape=jax.ShapeDtypeStruct((DEPTH, m, 6 * D_MODEL), F32),
        grid=(DEPTH, 6 * D_MODEL // tn),
        in_specs=[
            pl.BlockSpec((m, D_MODEL), lambda l, j: (0, 0)),
            pl.BlockSpec((1, D_MODEL, tn), lambda l, j: (l, 0, j)),
            pl.BlockSpec((1, 1, tn), lambda l, j: (l, 0, j)),
        ],
        out_specs=pl.BlockSpec((1, m, tn), lambda l, j: (l, 0, j)),
        compiler_params=_params(),
        name="ada_mod",
    )(c_all, ada_w, ada_b.reshape(DEPTH, 1, 6 * D_MODEL))


def _mod_spec(mod, tm, rows_per_group):
    if mod.ndim == 3:
        tiles = rows_per_group // tm
        return pl.BlockSpec((1, 1, D_MODEL), lambda i: (i // tiles, 0, 0))
    return pl.BlockSpec((tm, D_MODEL), lambda i: (i, 0))


def _mod_val(ref):
    return ref[0] if len(ref.shape) == 3 else ref[...]


def _rms_mod(x, g, sc, sh):
    ms = jnp.mean(x * x, axis=-1, keepdims=True)
    return (x * lax.rsqrt(ms + EPS) * g) * (1.0 + sc) + sh


def _inproj_kernel(x_ref, g_ref, sc_ref, sh_ref, w_ref, u_ref, u16_ref):
    h = _rms_mod(x_ref[...], g_ref[...], _mod_val(sc_ref), _mod_val(sh_ref))
    r = _mm(h.astype(BF), w_ref[...])
    u_ref[...] = r
    u16_ref[...] = r[:, 6 * GW:10 * GW].astype(BF)


def _inproj(x, g, sc, sh, w16, rows_per_group, tm):
    n = x.shape[0]
    return pl.pallas_call(
        _inproj_kernel,
        out_shape=(jax.ShapeDtypeStruct((n, IN_COLS), F32),
                   jax.ShapeDtypeStruct((n, 4 * GW), BF)),
        grid=(n // tm,),
        in_specs=[
            pl.BlockSpec((tm, D_MODEL), lambda i: (i, 0)),
            pl.BlockSpec((1, D_MODEL), lambda i: (0, 0)),
            _mod_spec(sc, tm, rows_per_group),
            _mod_spec(sh, tm, rows_per_group),
            pl.BlockSpec((D_MODEL, IN_COLS), lambda i: (0, 0)),
        ],
        out_specs=(pl.BlockSpec((tm, IN_COLS), lambda i: (i, 0)),
                   pl.BlockSpec((tm, 4 * GW), lambda i: (i, 0))),
        compiler_params=_params(),
        name="in_proj",
    )(x, g, sc, sh, w16)


def _hgrn_consts(c):
    r = np.arange(c)
    mats = [(r[None, :] <= r[:, None]), (r[None, :] > r[:, None])]
    masks = []
    w = c // 2
    while w >= 1:
        blk = r // (2 * w)
        half = (r // w) % 2
        mid = blk * 2 * w + w - 1
        t_rows = (r[None, :] > mid[:, None]) & (r[None, :] <= r[:, None])
        s_rows = (r[None, :] > r[:, None]) & (r[None, :] <= mid[:, None])
        mats.append(np.where(half[:, None] == 1, t_rows, s_rows))
        masks.append((blk[:, None] == blk[None, :]) & (half[:, None] == 1) & (half[None, :] == 0))
        w //= 2
    masks.append(r[:, None] == r[None, :])
    wmat = np.concatenate(mats, axis=0).astype(np.float32)
    return jnp.asarray(wmat, BF), jnp.asarray(np.stack(masks).astype(np.float32))


def _hgrn_kernel(q_ref, f_ref, i_ref, g_ref, lb_ref, hg_ref, w_ref, mk_ref, bd_ref, s0_ref,
                 o_ref, st_ref, *, groups, chunk, nlev):
    c = chunk

    @pl.when(pl.program_id(1) == 0)
    def _():
        st_ref[...] = s0_ref[...]

    lbb = jnp.clip(lb_ref[...], 0.0, 1.0)
    wmat = w_ref[...]
    for g in range(groups):
        a = f_ref[g]
        q = q_ref[g]
        iv = i_ref[g]
        f = lbb + (1.0 - lbb) * _sigmoid(a)
        lf = jnp.log(jnp.maximum(f, F_FLOOR))
        kk = (1.0 - lbb) * _sigmoid(-a)
        e = _mm_exact_lhs(wmat, lf)
        b = e[0:c]
        qb16 = (q * jnp.exp(b)).astype(BF)
        kr16 = (kk * jnp.exp(e[c:2 * c])).astype(BF)
        dec_last = jnp.exp(b[c - 1:c])
        iv16 = iv.astype(BF)
        qt = [(q * jnp.exp(e[(2 + l) * c:(3 + l) * c])).astype(BF) for l in range(nlev)]
        kt = [(kk * jnp.exp(e[(2 + l) * c:(3 + l) * c])).astype(BF) for l in range(nlev)]
        qt.append(q.astype(BF))
        kt.append(kk.astype(BF))
        st = st_ref[g]
        st16 = st.astype(BF)
        o_heads, st_heads = [], []
        for h in range(NH):
            sl = slice(h * HEAD_DIM, (h + 1) * HEAD_DIM)
            att = jnp.zeros((c, c), F32)
            for l in range(nlev + 1):
                att = att + mk_ref[l] * _mm_nt(qt[l][:, sl], kt[l][:, sl])
            o_heads.append(_mm(att.astype(BF), iv16[:, sl]) + _mm_nt(qb16[:, sl], st16[:, sl]))
            st_heads.append(st[:, sl] * dec_last[:, sl] + _mm_tn(iv16[:, sl], kr16[:, sl]))
        o = jnp.concatenate(o_heads, axis=1)
        st_ref[g] = jnp.concatenate(st_heads, axis=1)
        oh, ol = _split2(o * o)
        ms = _mm(oh, bd_ref[...]) + _mm(ol, bd_ref[...])
        o_ref[g] = o * lax.rsqrt(ms + EPS) * hg_ref[...] * _silu(g_ref[g])


def _hgrn(u3, lb, hg, s0t, chunk, groups):
    bsz, t, _ = u3.shape
    nlev = int(math.log2(chunk))
    wmat, masks = _hgrn_consts(chunk)
    bd = jnp.asarray(np.kron(np.eye(NH), np.full((HEAD_DIM, HEAD_DIM), 1.0 / HEAD_DIM)), BF)
    col = lambda k: pl.BlockSpec((groups, chunk, GW), lambda b, c: (b, c, k))
    const = lambda shape: pl.BlockSpec(shape, lambda b, c: (0,) * len(shape))
    return pl.pallas_call(
        functools.partial(_hgrn_kernel, groups=groups, chunk=chunk, nlev=nlev),
        out_shape=(jax.ShapeDtypeStruct((bsz, t, GW), F32),
                   jax.ShapeDtypeStruct((bsz, HEAD_DIM, GW), F32)),
        grid=(bsz // groups, t // chunk),
        in_specs=[col(0), col(1), col(2), col(3), const((1, GW)), const((1, GW)),
                  const(wmat.shape), const(masks.shape), const((GW, GW)),
                  pl.BlockSpec((groups, HEAD_DIM, GW), lambda b, c: (b, 0, 0))],
        out_specs=(pl.BlockSpec((groups, chunk, GW), lambda b, c: (b, c, 0)),
                   pl.BlockSpec((groups, HEAD_DIM, GW), lambda b, c: (b, 0, 0))),
        compiler_params=_params(),
        name="hgrn2",
    )(u3, u3, u3, u3, lb, hg, wmat, masks, bd, s0t)


def _rope_tables(pos):
    half = ROT_DIM // 2
    inv = np.exp(-math.log(ROPE_THETA) * np.arange(half, dtype=np.float32) * 2.0 / ROT_DIM)
    ang = jnp.asarray(pos, F32)[:, None] * jnp.asarray(inv)[None, :]
    cos, sin = jnp.cos(ang), jnp.sin(ang)
    n = ang.shape[0]
    ones = jnp.ones((n, HEAD_DIM - ROT_DIM), F32)
    zeros = jnp.zeros((n, HEAD_DIM - ROT_DIM), F32)
    ch = jnp.concatenate([cos, cos, ones], axis=1)
    sh = jnp.concatenate([-sin, sin, zeros], axis=1)
    return jnp.tile(ch, (1, NH)), jnp.tile(sh, (1, NH))


def _rope_kernel(q_ref, k_ref, cos_ref, sin_ref, qo_ref, ko_ref, k16_ref, km_ref):
    cos = cos_ref[...]
    sin = sin_ref[...]
    lane = lax.broadcasted_iota(jnp.int32, cos.shape, 1) % HEAD_DIM
    first = lane < ROT_DIM // 2

    def rot(x):
        partner = jnp.where(first, pltpu.roll(x, GW - ROT_DIM // 2, 1), pltpu.roll(x, ROT_DIM // 2, 1))
        return x * cos + partner * sin

    qo_ref[...] = rot(q_ref[...])
    k = rot(k_ref[...])
    ko_ref[...] = k
    k16_ref[...] = k.astype(BF)
    km_ref[0] = jnp.mean(k, axis=0, keepdims=True)


def _rope(u, cos, sin, nbatch, rows):
    n = u.shape[0]
    t = n // nbatch
    nt = t // rows
    return pl.pallas_call(
        _rope_kernel,
        out_shape=(jax.ShapeDtypeStruct((n, GW), F32), jax.ShapeDtypeStruct((n, GW), F32),
                   jax.ShapeDtypeStruct((n, GW), BF), jax.ShapeDtypeStruct((n // rows, 1, GW), F32)),
        grid=(nbatch, nt),
        in_specs=[pl.BlockSpec((rows, GW), lambda b, j: (b * nt + j, 4)),
                  pl.BlockSpec((rows, GW), lambda b, j: (b * nt + j, 5)),
                  pl.BlockSpec((rows, GW), lambda b, j: (j, 0)),
                  pl.BlockSpec((rows, GW), lambda b, j: (j, 0))],
        out_specs=(pl.BlockSpec((rows, GW), lambda b, j: (b * nt + j, 0)),
                   pl.BlockSpec((rows, GW), lambda b, j: (b * nt + j, 0)),
                   pl.BlockSpec((rows, GW), lambda b, j: (b * nt + j, 0)),
                   pl.BlockSpec((1, 1, GW), lambda b, j: (b * nt + j, 0, 0))),
        compiler_params=_params(),
        name="rope",
    )(u, u, cos, sin)


def _top3_mask(scores, valid, idx):
    cur = jnp.where(valid, scores, NEG_BIG)
    picked = jnp.zeros(scores.shape, F32)
    big = jnp.int32(1 << 30)
    for _ in range(MOBA_TOPK):
        mx = jnp.max(cur, axis=-1, keepdims=True)
        first = jnp.min(jnp.where(cur == mx, idx, big), axis=-1, keepdims=True)
        pick = idx == first
        picked = jnp.where(pick & valid, 1.0, picked)
        cur = jnp.where(pick, -jnp.inf, cur)
    return picked


def _moba_kernel(q_ref, k_ref, v_ref, km_ref, o_ref, m_ref, l_ref, acc_ref, sel_ref, *, nb):
    j = pl.program_id(1)
    own = (j * Q_BLOCK) // MOBA_BLOCK
    q = q_ref[...]
    q16 = q.astype(BF)
    tq = q.shape[0]
    col = lax.broadcasted_iota(jnp.int32, (tq, nb), 1)
    for h in range(NH):
        sl = slice(h * HEAD_DIM, (h + 1) * HEAD_DIM)
        s = _mm3_nt(q[:, sl], km_ref[:, sl])
        sel_ref[h] = _top3_mask(s, col < own, col)
    m_ref[...] = jnp.full(m_ref.shape, -jnp.inf, F32)
    l_ref[...] = jnp.zeros(l_ref.shape, F32)
    acc_ref[...] = jnp.zeros(acc_ref.shape, F32)

    def block(start, keep_fn):
        kblk = k_ref[pl.ds(start, MOBA_BLOCK), :]
        vblk = v_ref[pl.ds(start, MOBA_BLOCK), :]
        for h in range(NH):
            sl = slice(h * HEAD_DIM, (h + 1) * HEAD_DIM)
            s = _mm_nt(q16[:, sl], kblk[:, sl]) * SCALE
            s = jnp.where(keep_fn(h), s, NEG_BIG)
            m_old = m_ref[:, h:h + 1]
            m_new = jnp.maximum(m_old, jnp.max(s, axis=-1, keepdims=True))
            p = jnp.exp(s - m_new)
            alpha = jnp.exp(m_old - m_new)
            l_ref[:, h:h + 1] = alpha * l_ref[:, h:h + 1] + jnp.sum(p, axis=-1, keepdims=True)
            acc_ref[:, sl] = alpha * acc_ref[:, sl] + _mm(p.astype(BF), vblk[:, sl])
            m_ref[:, h:h + 1] = m_new

    def past(n, carry):
        def keep(h):
            return jnp.sum(jnp.where(col == n, sel_ref[h], 0.0), axis=-1, keepdims=True) > 0.5
        block(pl.multiple_of(n * MOBA_BLOCK, MOBA_BLOCK), keep)
        return carry

    lax.fori_loop(0, own, past, 0)

    start = pl.multiple_of(own * MOBA_BLOCK, MOBA_BLOCK)
    tpos = j * Q_BLOCK + lax.broadcasted_iota(jnp.int32, (tq, MOBA_BLOCK), 0)
    kpos = start + lax.broadcasted_iota(jnp.int32, (tq, MOBA_BLOCK), 1)
    causal = kpos <= tpos
    block(start, lambda h: causal)
    o_ref[...] = jnp.concatenate(
        [acc_ref[:, h * HEAD_DIM:(h + 1) * HEAD_DIM] / l_ref[:, h:h + 1] for h in range(NH)], axis=1)


def _moba_prompt(qrot, k16, u16, kmean, nbatch):
    n = qrot.shape[0]
    t = n // nbatch
    nq = t // Q_BLOCK
    nb = t // MOBA_BLOCK
    return pl.pallas_call(
        functools.partial(_moba_kernel, nb=nb),
        out_shape=jax.ShapeDtypeStruct((n, GW), F32),
        grid=(nbatch, nq),
        in_specs=[pl.BlockSpec((Q_BLOCK, GW), lambda b, j: (b * nq + j, 0)),
                  pl.BlockSpec((t, GW), lambda b, j: (b, 0)),
                  pl.BlockSpec((t, GW), lambda b, j: (b, 0)),
                  pl.BlockSpec((nb, GW), lambda b, j: (b, 0))],
        out_specs=pl.BlockSpec((Q_BLOCK, GW), lambda b, j: (b * nq + j, 0)),
        scratch_shapes=[pltpu.VMEM((Q_BLOCK, LANES), F32), pltpu.VMEM((Q_BLOCK, LANES), F32),
                        pltpu.VMEM((Q_BLOCK, GW), F32), pltpu.VMEM((NH, Q_BLOCK, nb), F32)],
        compiler_params=_params(),
        name="moba_prompt",
    )(qrot, k16, u16, kmean)


def _strict_upper(n):
    r = np.arange(n)
    return jnp.asarray((r[:, None] > r[None, :]).astype(np.float32), BF)


def _sb_kernel(q_ref, k_ref, v_ref, up_ref, o_ref, carry_ref, acc_ref):
    j = pl.program_id(1)
    q16 = q_ref[...].astype(BF)
    tq = q16.shape[0]
    upper = up_ref[...]
    carry_ref[...] = jnp.zeros(carry_ref.shape, F32)
    acc_ref[...] = jnp.zeros(acc_ref.shape, F32)

    def block(start, mask):
        kblk = k_ref[pl.ds(start, Q_BLOCK), :]
        vblk = v_ref[pl.ds(start, Q_BLOCK), :]
        for h in range(NH):
            sl = slice(h * HEAD_DIM, (h + 1) * HEAD_DIM)
            z = _mm_nt(q16[:, sl], kblk[:, sl]) * SCALE
            lpos, lneg = _log_sigmoid_pair(z)
            if mask is not None:
                lneg = jnp.where(mask, lneg, 0.0)
            hi, lo = _split2(lneg)
            rest = carry_ref[:, h:h + 1] + _mm(hi, upper) + _mm(lo, upper)
            a = jnp.exp(lpos + jnp.minimum(rest, 0.0))
            if mask is not None:
                a = jnp.where(mask, a, 0.0)
            acc_ref[:, sl] = acc_ref[:, sl] + _mm(a.astype(BF), vblk[:, sl])
            carry_ref[:, h:h + 1] = carry_ref[:, h:h + 1] + jnp.sum(lneg, axis=-1, keepdims=True)

    row = lax.broadcasted_iota(jnp.int32, (tq, Q_BLOCK), 0)
    colk = lax.broadcasted_iota(jnp.int32, (tq, Q_BLOCK), 1)
    block(pl.multiple_of(j * Q_BLOCK, Q_BLOCK), colk < row)

    def past(i, carry):
        block(pl.multiple_of((j - 1 - i) * Q_BLOCK, Q_BLOCK), None)
        return carry

    lax.fori_loop(0, j, past, 0)
    o_ref[...] = acc_ref[...]


def _sb_prompt(u, u16, nbatch):
    n = u.shape[0]
    t = n // nbatch
    nq = t // Q_BLOCK
    return pl.pallas_call(
        _sb_kernel,
        out_shape=jax.ShapeDtypeStruct((n, GW), F32),
        grid=(nbatch, nq),
        in_specs=[pl.BlockSpec((Q_BLOCK, GW), lambda b, j: (b * nq + j, 7)),
                  pl.BlockSpec((t, GW), lambda b, j: (b, 2)),
                  pl.BlockSpec((t, GW), lambda b, j: (b, 3)),
                  pl.BlockSpec((Q_BLOCK, Q_BLOCK), lambda b, j: (0, 0))],
        out_specs=pl.BlockSpec((Q_BLOCK, GW), lambda b, j: (b * nq + j, 0)),
        scratch_shapes=[pltpu.VMEM((Q_BLOCK, LANES), F32), pltpu.VMEM((Q_BLOCK, GW), F32)],
        compiler_params=_params(),
        name="sb_prompt",
    )(u, u16, u16, _strict_upper(Q_BLOCK))


def _block_diag_q(q, nbatch, t):
    q4 = q.reshape(nbatch, t, NH, HEAD_DIM).transpose(0, 2, 3, 1)
    eye = jnp.eye(NH, dtype=q.dtype)
    bd = q4[:, :, :, None, :] * eye[None, :, None, :, None]
    bd = bd.reshape(nbatch, GW, NH * t)
    return jnp.pad(bd, ((0, 0), (0, 0), (0, LANES - NH * t)))


def _page_specs(layer, n_phys, n_pages):
    def spec(p):
        return pl.BlockSpec((1, PAGE_SIZE, GW), lambda b, pt: (layer * n_phys + pt[b * n_pages + p], 0, 0))
    return [spec(p) for p in range(n_pages)]


def _heads_out(acc, den, t):
    return jnp.concatenate(
        [acc[h * t:(h + 1) * t, h * HEAD_DIM:(h + 1) * HEAD_DIM] / den[h * t:(h + 1) * t, 0:HEAD_DIM]
         for h in range(NH)], axis=1)


def _moba_dec_kernel(pt_ref, qbd_ref, knew_ref, vnew_ref, *refs, n_pages, t, own):
    del pt_ref
    k_refs, v_refs, o_ref = refs[:n_pages], refs[n_pages:2 * n_pages], refs[2 * n_pages]
    qbd = qbd_ref[0]
    qbd16 = qbd.astype(BF)
    pages_per_block = MOBA_BLOCK // PAGE_SIZE
    kpages = [k_refs[p][0] for p in range(n_pages)]
    lg = [_mm(kp.astype(BF), qbd16) * SCALE for kp in kpages]
    sums = [jnp.sum(kp, axis=0, keepdims=True) for kp in kpages]
    means = []
    for n in range(own):
        acc = sums[n * pages_per_block]
        for r in range(1, pages_per_block):
            acc = acc + sums[n * pages_per_block + r]
        means.append(acc * (1.0 / MOBA_BLOCK))
    kmean = jnp.concatenate(means, axis=0)
    scores = _mm3(kmean, qbd)
    rowi = lax.broadcasted_iota(jnp.int32, scores.shape, 0)
    cur = scores
    sel = jnp.zeros(scores.shape, F32)
    for _ in range(min(MOBA_TOPK, own)):
        mx = jnp.max(cur, axis=0, keepdims=True)
        first = jnp.min(jnp.where(cur == mx, rowi, jnp.int32(1 << 30)), axis=0, keepdims=True)
        pick = rowi == first
        sel = jnp.where(pick, 1.0, sel)
        cur = jnp.where(pick, -jnp.inf, cur)
    logits = []
    for p in range(n_pages):
        n = p // pages_per_block
        keep = sel[n:n + 1, :] > 0.5
        logits.append(jnp.where(keep, lg[p], NEG_BIG))
    knew = knew_ref[...]
    lnew = _mm(knew.astype(BF), qbd16) * SCALE
    keyi = lax.broadcasted_iota(jnp.int32, lnew.shape, 0)
    qi = lax.broadcasted_iota(jnp.int32, lnew.shape, 1) % t
    lnew = jnp.where(keyi <= qi, lnew, NEG_BIG)
    m = jnp.max(lnew, axis=0, keepdims=True)
    for lp in logits:
        m = jnp.maximum(m, jnp.max(lp, axis=0, keepdims=True))
    ones = jnp.ones((PAGE_SIZE, GW), BF)
    enew = jnp.exp(lnew - m)
    enew_t = jnp.transpose(jnp.concatenate([enew, jnp.zeros((PAGE_SIZE - t, LANES), F32)], axis=0)).astype(BF)
    vnew = jnp.concatenate([vnew_ref[...], jnp.zeros((PAGE_SIZE - t, GW), F32)], axis=0).astype(BF)
    acc = _mm(enew_t, vnew)
    den = _mm(enew_t, ones)
    for p in range(n_pages):
        e_t = jnp.transpose(jnp.exp(logits[p] - m)).astype(BF)
        acc = acc + _mm(e_t, v_refs[p][0].astype(BF))
        den = den + _mm(e_t, ones)
    o_ref[...] = _heads_out(acc, den, t)


def _sb_dec_kernel(pt_ref, qbd_ref, knew_ref, vnew_ref, up_ref, *refs, n_pages, t):
    del pt_ref
    k_refs, v_refs, o_ref = refs[:n_pages], refs[n_pages:2 * n_pages], refs[2 * n_pages]
    qbd16 = qbd_ref[0].astype(BF)
    upper_t = up_ref[...]

    def block(kblk, vblk, carry, acc, mask):
        z = _mm(kblk.astype(BF), qbd16) * SCALE
        lpos, lneg = _log_sigmoid_pair(z)
        if mask is not None:
            lneg = jnp.where(mask, lneg, 0.0)
        hi, lo = _split2(lneg)
        rest = carry + _mm(upper_t, hi) + _mm(upper_t, lo)
        a = jnp.exp(lpos + jnp.minimum(rest, 0.0))
        if mask is not None:
            a = jnp.where(mask, a, 0.0)
        acc = acc + _mm(jnp.transpose(a).astype(BF), vblk.astype(BF))
        return carry + jnp.sum(lneg, axis=0, keepdims=True), acc

    pad = jnp.zeros((PAGE_SIZE - t, GW), F32)
    knew = jnp.concatenate([knew_ref[...], pad], axis=0)
    vnew = jnp.concatenate([vnew_ref[...], pad], axis=0)
    keyi = lax.broadcasted_iota(jnp.int32, (PAGE_SIZE, LANES), 0)
    qi = lax.broadcasted_iota(jnp.int32, (PAGE_SIZE, LANES), 1) % t
    carry, acc = block(knew, vnew, jnp.zeros((1, LANES), F32), jnp.zeros((LANES, GW), F32), keyi < qi)
    for p in range(n_pages - 1, -1, -1):
        carry, acc = block(k_refs[p][0], v_refs[p][0], carry, acc, None)
    o_ref[...] = _heads_out(acc, jnp.ones((LANES, GW), F32), t)


def _dec_attention(kind, layer, page_table, qbd, knew, vnew, kcache, vcache, t, pos0):
    nbatch, n_pages = page_table.shape
    n_phys = kcache.shape[0] // DEPTH
    pspecs = _page_specs(layer, n_phys, n_pages)
    row = pl.BlockSpec((t, GW), lambda b, pt: (b, 0))
    in_specs = [pl.BlockSpec((1, GW, LANES), lambda b, pt: (b, 0, 0)), row, row]
    args = [qbd, knew, vnew]
    if kind == "moba":
        body = functools.partial(_moba_dec_kernel, n_pages=n_pages, t=t, own=pos0 // MOBA_BLOCK)
    else:
        body = functools.partial(_sb_dec_kernel, n_pages=n_pages, t=t)
        in_specs.append(pl.BlockSpec((PAGE_SIZE, PAGE_SIZE), lambda b, pt: (0, 0)))
        r = np.arange(PAGE_SIZE)
        args.append(jnp.asarray((r[None, :] > r[:, None]).astype(np.float32), BF))
    in_specs += pspecs + pspecs
    args += [kcache] * n_pages + [vcache] * n_pages
    return pl.pallas_call(
        body,
        out_shape=jax.ShapeDtypeStruct((nbatch * t, GW), F32),
        grid_spec=pltpu.PrefetchScalarGridSpec(
            num_scalar_prefetch=1, grid=(nbatch,), in_specs=in_specs,
            out_specs=pl.BlockSpec((t, GW), lambda b, pt: (b, 0))),
        compiler_params=_params(),
        name=kind + "_decode",
    )(page_table.reshape(-1), *args)


POOL_HALO = 16


def _pool_bands(seqs, t):
    halo_seq = np.repeat(np.arange(seqs), POOL_HALO)
    halo_e = np.tile(np.arange(POOL_HALO), seqs)
    row_seq = np.repeat(np.arange(seqs), t)
    row_e = np.tile(np.arange(t), seqs) + POOL_HALO
    col_seq = np.concatenate([halo_seq, row_seq])[None, :]
    col_e = np.concatenate([halo_e, row_e])[None, :]
    same = col_seq == row_seq[:, None]
    r = row_e[:, None]
    return jnp.asarray(np.stack([(same & (col_e <= r) & (col_e > r - w)).astype(np.float32)
                                 for w in POOL_WINDOWS]), BF)


def _pool_kernel(u_ref, halo_ref, buf_ref, band_ref, pw_ref, ps_ref, o_ref, *, rows, nt, t_seq, pos0):
    i = pl.program_id(0)
    u = u_ref[...]
    halo = jnp.where(i % nt == 0, buf_ref[...], halo_ref[...])
    ext = jnp.concatenate([halo, u], axis=0)
    hi, mid, lo = _split3(ext)
    tok = (i * rows + lax.broadcasted_iota(jnp.int32, (rows, 1), 0)) % t_seq
    pos = (pos0 + tok).astype(F32)
    parts = []
    gc = GW // len(POOL_WINDOWS)
    for gi, w in enumerate(POOL_WINDOWS):
        sl = slice(gi * gc, (gi + 1) * gc)
        band = band_ref[gi]
        ssum = _mm(band, hi[:, sl]) + _mm(band, mid[:, sl]) + _mm(band, lo[:, sl])
        cnt = jnp.minimum(float(w), pos + 1.0)
        parts.append(ssum / cnt - u[:, sl])
    d = jnp.concatenate(parts, axis=1)
    o_ref[...] = _mm(d.astype(BF), pw_ref[...]) * ps_ref[...]


def _pool(u, buf_flat, pw_bd16, ps, nbatch, t, pos0):
    n = u.shape[0]
    if t >= MOBA_BLOCK:
        rows, seqs, hr = 256, 1, POOL_HALO
        nt = t // rows
        band = _pool_bands(1, rows)
        halo_arr = u
        halo_spec = pl.BlockSpec((hr, GW), lambda i: (jnp.maximum(i * (rows // hr) - 1, 0), 10))
        buf_spec = pl.BlockSpec((hr, GW), lambda i: (i // nt, 0))
    else:
        seqs = 128 // t
        rows, hr, nt = seqs * t, seqs * POOL_HALO, 1
        band = _pool_bands(seqs, t)
        halo_arr = buf_flat
        halo_spec = pl.BlockSpec((hr, GW), lambda i: (i, 0))
        buf_spec = halo_spec
    return pl.pallas_call(
        functools.partial(_pool_kernel, rows=rows, nt=nt, t_seq=t, pos0=pos0),
        out_shape=jax.ShapeDtypeStruct((n, GW), F32),
        grid=(n // rows,),
        in_specs=[pl.BlockSpec((rows, GW), lambda i: (i, 10)), halo_spec, buf_spec,
                  pl.BlockSpec(band.shape, lambda i: (0, 0, 0)),
                  pl.BlockSpec((GW, GW), lambda i: (0, 0)),
                  pl.BlockSpec((1, GW), lambda i: (0, 0))],
        out_specs=pl.BlockSpec((rows, GW), lambda i: (i, 0)),
        compiler_params=_params(),
        name="pool_mixer",
    )(u, halo_arr, buf_flat, band, pw_bd16, ps)


def _route(logits):
    lane = lax.broadcasted_iota(jnp.int32, logits.shape, 1)
    big = jnp.int32(1 << 30)
    gmask = lane < N_GROUPS
    gl = jnp.where(gmask, logits, -jnp.inf)
    gmax = jnp.max(gl, axis=-1, keepdims=True)
    gsum = jnp.sum(jnp.where(gmask, jnp.exp(gl - gmax), 0.0), axis=-1, keepdims=True)
    g_top = 1.0 / gsum
    g_idx = jnp.min(jnp.where(gl == gmax, lane, big), axis=-1, keepdims=True)
    emask = (lane >= N_GROUPS) & (lane < N_GROUPS + N_EXPERTS) & ((lane - N_GROUPS) // EXP_PER_GROUP == g_idx)
    el = jnp.where(emask, logits, -jnp.inf)
    m1 = jnp.max(el, axis=-1, keepdims=True)
    i1 = jnp.min(jnp.where(el == m1, lane, big), axis=-1, keepdims=True)
    el2 = jnp.where(lane == i1, -jnp.inf, el)
    m2 = jnp.max(el2, axis=-1, keepdims=True)
    i2 = jnp.min(jnp.where(el2 == m2, lane, big), axis=-1, keepdims=True)
    r = jnp.exp(m2 - m1)
    w1 = g_top / (1.0 + r)
    w2 = g_top * r / (1.0 + r)
    return jnp.where(lane == i1, w1, jnp.where(lane == i2, w2, 0.0))


def _outproj_kernel(oa_ref, ob_ref, oc_ref, od_ref, x_ref, gt_ref, g_ref, sc_ref, sh_ref, w_ref,
                    wr_ref, br_ref, x1_ref, h16_ref, gate_ref):
    merged = jnp.concatenate([oa_ref[...], ob_ref[...], oc_ref[...], od_ref[...]], axis=1).astype(BF)
    x1 = x_ref[...] + _mod_val(gt_ref) * _mm(merged, w_ref[...])
    x1_ref[...] = x1
    h = _rms_mod(x1, g_ref[...], _mod_val(sc_ref), _mod_val(sh_ref))
    h16_ref[...] = h.astype(BF)
    gate_ref[...] = _route(_mm3(h, wr_ref[...]) + br_ref[...])


def _outproj(oa, ob, oc, od, x, gt, g, sc, sh, w16, wr, br, rows_per_group, tm):
    n = x.shape[0]
    quarter = pl.BlockSpec((tm, GW), lambda i: (i, 0))
    full = pl.BlockSpec((tm, D_MODEL), lambda i: (i, 0))
    return pl.pallas_call(
        _outproj_kernel,
        out_shape=(jax.ShapeDtypeStruct((n, D_MODEL), F32), jax.ShapeDtypeStruct((n, D_MODEL), BF),
                   jax.ShapeDtypeStruct((n, LANES), F32)),
        grid=(n // tm,),
        in_specs=[quarter, quarter, quarter, quarter, full,
                  _mod_spec(gt, tm, rows_per_group),
                  pl.BlockSpec((1, D_MODEL), lambda i: (0, 0)),
                  _mod_spec(sc, tm, rows_per_group), _mod_spec(sh, tm, rows_per_group),
                  pl.BlockSpec((D_MODEL, D_MODEL), lambda i: (0, 0)),
                  pl.BlockSpec((D_MODEL, LANES), lambda i: (0, 0)),
                  pl.BlockSpec((1, LANES), lambda i: (0, 0))],
        out_specs=(full, full, pl.BlockSpec((tm, LANES), lambda i: (i, 0))),
        compiler_params=_params(),
        name="out_proj_router",
    )(oa, ob, oc, od, x, gt, g, sc, sh, w16, wr, br)


def _moe_kernel(h_ref, gate_ref, x1_ref, gt_ref, wg_ref, wu_ref, wd_ref, o_ref, acc_ref):
    e = pl.program_id(1)

    @pl.when(e == 0)
    def _():
        acc_ref[...] = jnp.zeros(acc_ref.shape, F32)

    h = h_ref[...]
    a = _silu(_mm(h, wg_ref[0])) * _mm(h, wu_ref[0])
    lane = lax.broadcasted_iota(jnp.int32, gate_ref.shape, 1)
    gcol = jnp.sum(jnp.where(lane == e + N_GROUPS, gate_ref[...], 0.0), axis=-1, keepdims=True)
    acc_ref[...] += gcol * _mm(a.astype(BF), wd_ref[0])

    @pl.when(e == N_EXPERTS - 1)
    def _():
        o_ref[...] = x1_ref[...] + _mod_val(gt_ref) * acc_ref[...]


def _moe(h16, gate, x1, gt, wg16, wu16, wd16, layer, rows_per_group, tm):
    n = x1.shape[0]
    full = lambda: pl.BlockSpec((tm, D_MODEL), lambda i, e: (i, 0))
    if gt.ndim == 3:
        tiles = rows_per_group // tm
        gt_spec = pl.BlockSpec((1, 1, D_MODEL), lambda i, e: (i // tiles, 0, 0))
    else:
        gt_spec = pl.BlockSpec((tm, D_MODEL), lambda i, e: (i, 0))
    return pl.pallas_call(
        _moe_kernel,
        out_shape=jax.ShapeDtypeStruct((n, D_MODEL), F32),
        grid=(n // tm, N_EXPERTS),
        in_specs=[full(), pl.BlockSpec((tm, LANES), lambda i, e: (i, 0)), full(), gt_spec,
                  pl.BlockSpec((1, D_MODEL, D_EXPERT), lambda i, e: (layer * N_EXPERTS + e, 0, 0)),
                  pl.BlockSpec((1, D_MODEL, D_EXPERT), lambda i, e: (layer * N_EXPERTS + e, 0, 0)),
                  pl.BlockSpec((1, D_EXPERT, D_MODEL), lambda i, e: (layer * N_EXPERTS + e, 0, 0))],
        out_specs=full(),
        scratch_shapes=[pltpu.VMEM((tm, D_MODEL), F32)],
        compiler_params=_params(),
        name="moe_experts",
    )(h16, gate, x1, gt, wg16, wu16, wd16)


def _final_kernel(x_ref, g_ref, o_ref):
    x = x_ref[...]
    ms = jnp.mean(x * x, axis=-1, keepdims=True)
    o_ref[...] = x * lax.rsqrt(ms + EPS) * g_ref[...]


def _final_norm(x, g, tm):
    n = x.shape[0]
    return pl.pallas_call(
        _final_kernel,
        out_shape=jax.ShapeDtypeStruct((n, D_MODEL), F32),
        grid=(n // tm,),
        in_specs=[pl.BlockSpec((tm, D_MODEL), lambda i: (i, 0)), pl.BlockSpec((1, D_MODEL), lambda i: (0, 0))],
        out_specs=pl.BlockSpec((tm, D_MODEL), lambda i: (i, 0)),
        compiler_params=_params(),
        name="final_norm",
    )(x, g)


def _state_in(s):
    b = s.shape[0]
    return s.transpose(0, 3, 1, 2).reshape(b, HEAD_DIM, GW)


def _state_out(s):
    b = s.shape[0]
    return s.reshape(b, HEAD_DIM, NH, HEAD_DIM).transpose(0, 2, 3, 1)


def _hgrn_lower_bounds(lb_logits):
    p = jax.nn.softmax(lb_logits.astype(F32), axis=0)
    return jnp.cumsum(p, axis=0) - p[0:1]


def _layer(x, mods, wl, layer, nbatch, t, pos0, s0, pbuf, past, tm):
    sh1, sc1, gt1, sh2, sc2, gt2 = mods
    n = nbatch * t
    u, u16 = _inproj(x, wl["g_mix"], sc1, sh1, wl["w_in"], t, tm)
    u3 = u.reshape(nbatch, t, IN_COLS)

    chunk = HG_CHUNK if t % HG_CHUNK == 0 else t
    groups = 2 if t % HG_CHUNK == 0 else 8
    oa, st = _hgrn(u3, wl["lb"], wl["hg_g"], _state_in(s0), chunk, groups)

    if past is None:
        cos, sin = _rope_tables(np.arange(t) + pos0)
        qrot, krot, krot16, kmean = _rope(u, cos, sin, nbatch, MOBA_BLOCK)
        ob = _moba_prompt(qrot, krot16, u16, kmean.reshape(-1, GW), nbatch)
        oc = _sb_prompt(u, u16, nbatch)
    else:
        cos, sin = _rope_tables(np.tile(np.arange(t), nbatch) + pos0)
        qrot, krot, _, _ = _rope(u, cos, sin, 1, n)
        page_table, cmk, cmv, csk, csv = past
        vb = u[:, 6 * GW:7 * GW]
        ob = _dec_attention("moba", layer, page_table, _block_diag_q(qrot, nbatch, t), krot, vb, cmk, cmv, t, pos0)
        qc, kc, vc = u[:, 7 * GW:8 * GW], u[:, 8 * GW:9 * GW], u[:, 9 * GW:10 * GW]
        oc = _dec_attention("sb", layer, page_table, _block_diag_q(qc, nbatch, t), kc, vc, csk, csv, t, pos0)

    buf_flat = jnp.pad(pbuf, ((0, 0), (POOL_HALO - POOL_BUF, 0), (0, 0))).reshape(nbatch * POOL_HALO, GW)
    od = _pool(u, buf_flat, wl["pool_w"], wl["pool_scale"], nbatch, t, pos0)
    ud = u3[:, :, 10 * GW:]
    buf_new = jnp.concatenate([pbuf, ud], axis=1)[:, -POOL_BUF:]

    x1, h16, gate = _outproj(oa.reshape(n, GW), ob, oc, od, x, gt1, wl["g_ffn"], sc2, sh2,
                             wl["w_out"], wl["wr"], wl["br"], t, tm)
    x2 = _moe(h16, gate, x1, gt2, wl["ewg"], wl["ewu"], wl["ewd"], layer, t, tm)

    rows = lambda z: z.reshape(nbatch, t, NH, HEAD_DIM)
    states = (rows(krot), rows(u[:, 6 * GW:7 * GW]), rows(u[:, 8 * GW:9 * GW]), rows(u[:, 9 * GW:10 * GW]),
              _state_out(st), buf_new)
    return x2, states


def kernel(x_prompt, x_sample, cache_moba_k, cache_moba_v, cache_sb_k, cache_sb_v, state_hgrn, state_pool,
           page_table, c_prompt, c_sample, ada_w, ada_b, norm_mix_g, norm_ffn_g, w_in, hgrn_lb, hgrn_norm_g,
           pool_w, pool_scale, w_out, router_group_w, router_group_b, router_expert_w, router_expert_b,
           expert_w_gate, expert_w_up, expert_w_down, final_norm_g):
    bp, tp, _ = x_prompt.shape
    bs, ts, _ = x_sample.shape
    pos0_s = page_table.shape[1] * PAGE_SIZE
    n_phys = cache_moba_k.shape[1]

    c_all = jnp.concatenate([c_prompt, c_sample], axis=0)
    mpad = (-c_all.shape[0]) % 8
    mod = _ada_all(jnp.pad(c_all, ((0, mpad), (0, 0))), ada_w, ada_b)

    lbs = _hgrn_lower_bounds(hgrn_lb)
    w_in16 = w_in.astype(BF)
    w_out16 = w_out.astype(BF)
    ewg16 = expert_w_gate.astype(BF).reshape(DEPTH * N_EXPERTS, D_MODEL, D_EXPERT)
    ewu16 = expert_w_up.astype(BF).reshape(DEPTH * N_EXPERTS, D_MODEL, D_EXPERT)
    ewd16 = expert_w_down.astype(BF).reshape(DEPTH * N_EXPERTS, D_EXPERT, D_MODEL)
    wr = jnp.concatenate([router_group_w, router_expert_w], axis=-1)
    wr = jnp.pad(wr, ((0, 0), (0, 0), (0, LANES - wr.shape[-1])))
    br = jnp.concatenate([router_group_b, router_expert_b], axis=-1)
    br = jnp.pad(br, ((0, 0), (0, LANES - br.shape[-1])))
    caches = tuple(c.reshape(DEPTH * n_phys, PAGE_SIZE, GW)
                   for c in (cache_moba_k, cache_moba_v, cache_sb_k, cache_sb_v))

    xp = x_prompt.reshape(bp * tp, D_MODEL)
    xs = x_sample.reshape(bs * ts, D_MODEL)
    st_p, st_s = [], []
    for l in range(DEPTH):
        pw_bd = jnp.zeros((GW, GW), F32)
        gc = GW // len(POOL_WINDOWS)
        for gi in range(len(POOL_WINDOWS)):
            pw_bd = lax.dynamic_update_slice(pw_bd, pool_w[l, gi], (gi * gc, gi * gc))
        wl = dict(g_mix=norm_mix_g[l][None], g_ffn=norm_ffn_g[l][None], w_in=w_in16[l], lb=lbs[l][None],
                  hg_g=hgrn_norm_g[l][None], pool_w=pw_bd.astype(BF), pool_scale=pool_scale[l][None],
                  w_out=w_out16[l], wr=wr[l], br=br[l][None], ewg=ewg16, ewu=ewu16, ewd=ewd16)
        mp = mod[l, :bp].reshape(bp, 6, 1, D_MODEL)
        mods_p = tuple(mp[:, k] for k in range(6))
        ms = jnp.repeat(mod[l, bp:bp + bs], ts, axis=0).reshape(bs * ts, 6, D_MODEL)
        mods_s = tuple(ms[:, k] for k in range(6))
        xp, sp = _layer(xp, mods_p, wl, l, bp, tp, 0, jnp.zeros((bp, NH, HEAD_DIM, HEAD_DIM), F32),
                        jnp.zeros((bp, POOL_BUF, GW), F32), None, min(ROW_TILE, tp))
        xs, ss = _layer(xs, mods_s, wl, l, bs, ts, pos0_s, state_hgrn[l], state_pool[l],
                        (page_table,) + caches, min(ROW_TILE, bs * ts))
        st_p.append(sp)
        st_s.append(ss)

    y_prompt = _final_norm(xp, final_norm_g[None], min(ROW_TILE, tp)).reshape(bp, tp, D_MODEL)
    y_sample = _final_norm(xs, final_norm_g[None], min(ROW_TILE, bs * ts)).reshape(bs, ts, D_MODEL)
    stk = lambda sts, i: jnp.stack([s[i] for s in sts], axis=0)
    return (y_prompt, y_sample,
            stk(st_p, 0), stk(st_p, 1), stk(st_p, 2), stk(st_p, 3), stk(st_p, 4), stk(st_p, 5),
            stk(st_s, 0), stk(st_s, 1), stk(st_s, 2), stk(st_s, 3), stk(st_s, 4), stk(st_s, 5))
```

```python
import functools
import math

import numpy as np
import jax
import jax.numpy as jnp
from jax import lax
from jax.experimental import pallas as pl
from jax.experimental.pallas import tpu as pltpu

F32 = jnp.float32
BF = jnp.bfloat16

D_MODEL = 1024
DEPTH = 4
PAGE_SIZE = 128
HEAD_DIM = 64
GW = 256
NH = 4
HG_CHUNK = 64
MOBA_BLOCK = 256
MOBA_TOPK = 3
ROT_DIM = 16
ROPE_THETA = 500000.0
POOL_WINDOWS = (2, 4, 8, 16)
POOL_BUF = 15
N_GROUPS = 4
EXP_PER_GROUP = 4
N_EXPERTS = 16
D_EXPERT = 512
IN_COLS = 11 * GW
EPS = 1e-6
NEG_BIG = -1e30
F_FLOOR = 1e-30
SCALE = 1.0 / math.sqrt(HEAD_DIM)

V7X_VMEM_BYTES = 64 * 1024 * 1024
VMEM_LIMIT = V7X_VMEM_BYTES - 12 * 1024 * 1024
LANES = 128
ROW_TILE = 512


def _params():
    return pltpu.CompilerParams(vmem_limit_bytes=VMEM_LIMIT)


def _mm(a, b):
    return jnp.dot(a, b, preferred_element_type=F32)


def _mm_nt(a, b):
    return lax.dot_general(a, b, (((1,), (1,)), ((), ())), preferred_element_type=F32)


def _mm_tn(a, b):
    return lax.dot_general(a, b, (((0,), (0,)), ((), ())), preferred_element_type=F32)


def _split2(x):
    hi = x.astype(BF)
    lo = (x - hi.astype(F32)).astype(BF)
    return hi, lo


def _split3(x):
    hi = x.astype(BF)
    r = x - hi.astype(F32)
    mid = r.astype(BF)
    lo = (r - mid.astype(F32)).astype(BF)
    return hi, mid, lo


def _mm_exact_lhs(w01, x):
    hi, mid, lo = _split3(x)
    return _mm(w01, hi) + _mm(w01, mid) + _mm(w01, lo)


def _mm3_nt(a, b):
    ah, al = _split2(a)
    bh, bl = _split2(b)
    return _mm_nt(ah, bh) + _mm_nt(ah, bl) + _mm_nt(al, bh)


def _mm3(a, b):
    ah, al = _split2(a)
    bh, bl = _split2(b)
    return _mm(ah, bh) + _mm(ah, bl) + _mm(al, bh)


def _sigmoid(x):
    return 1.0 / (1.0 + jnp.exp(-x))


def _silu(x):
    return x * _sigmoid(x)


def _log_sigmoid_pair(z):
    t = jnp.log(1.0 + jnp.exp(-jnp.abs(z)))
    return jnp.minimum(z, 0.0) - t, -jnp.maximum(z, 0.0) - t


def _ada_kernel(c_ref, w_ref, b_ref, o_ref):
    o_ref[0] = _mm(c_ref[...].astype(BF), w_ref[0].astype(BF)) + b_ref[0]


def _ada_all(c_all, ada_w, ada_b):
    m = c_all.shape[0]
    tn = 1536
    return pl.pallas_call(
        _ada_kernel,
        out_shape=jax.ShapeDtypeStruct((DEPTH, m, 6 * D_MODEL), F32),
        grid=(DEPTH, 6 * D_MODEL // tn),
        in_specs=[
            pl.BlockSpec((m, D_MODEL), lambda l, j: (0, 0)),
            pl.BlockSpec((1, D_MODEL, tn), lambda l, j: (l, 0, j)),
            pl.BlockSpec((1, 1, tn), lambda l, j: (l, 0, j)),
        ],
        out_specs=pl.BlockSpec((1, m, tn), lambda l, j: (l, 0, j)),
        compiler_params=_params(),
        name="ada_mod",
    )(c_all, ada_w, ada_b.reshape(DEPTH, 1, 6 * D_MODEL))


def _mod_spec(mod, tm, rows_per_group):
    if mod.ndim == 3:
        tiles = rows_per_group // tm
        return pl.BlockSpec((1, 1, D_MODEL), lambda i: (i // tiles, 0, 0))
    return pl.BlockSpec((tm, D_MODEL), lambda i: (i, 0))


def _mod_val(ref):
    return ref[0] if len(ref.shape) == 3 else ref[...]


def _rms_mod(x, g, sc, sh):
    ms = jnp.mean(x * x, axis=-1, keepdims=True)
    return (x * lax.rsqrt(ms + EPS) * g) * (1.0 + sc) + sh


def _inproj_kernel(x_ref, g_ref, sc_ref, sh_ref, w_ref, u_ref, u16_ref):
    h = _rms_mod(x_ref[...], g_ref[...], _mod_val(sc_ref), _mod_val(sh_ref))
    r = _mm(h.astype(BF), w_ref[...])
    u_ref[...] = r
    u16_ref[...] = r[:, 6 * GW:10 * GW].astype(BF)


def _inproj(x, g, sc, sh, w16, rows_per_group, tm):
    n = x.shape[0]
    return pl.pallas_call(
        _inproj_kernel,
        out_shape=(jax.ShapeDtypeStruct((n, IN_COLS), F32),
                   jax.ShapeDtypeStruct((n, 4 * GW), BF)),
        grid=(n // tm,),
        in_specs=[
            pl.BlockSpec((tm, D_MODEL), lambda i: (i, 0)),
            pl.BlockSpec((1, D_MODEL), lambda i: (0, 0)),
            _mod_spec(sc, tm, rows_per_group),
            _mod_spec(sh, tm, rows_per_group),
            pl.BlockSpec((D_MODEL, IN_COLS), lambda i: (0, 0)),
        ],
        out_specs=(pl.BlockSpec((tm, IN_COLS), lambda i: (i, 0)),
                   pl.BlockSpec((tm, 4 * GW), lambda i: (i, 0))),
        compiler_params=_params(),
        name="in_proj",
    )(x, g, sc, sh, w16)


def _hgrn_consts(c):
    r = np.arange(c)
    mats = [(r[None, :] <= r[:, None]), (r[None, :] > r[:, None])]
    masks = []
    w = c // 2
    while w >= 1:
        blk = r // (2 * w)
        half = (r // w) % 2
        mid = blk * 2 * w + w - 1
        t_rows = (r[None, :] > mid[:, None]) & (r[None, :] <= r[:, None])
        s_rows = (r[None, :] > r[:, None]) & (r[None, :] <= mid[:, None])
        mats.append(np.where(half[:, None] == 1, t_rows, s_rows))
        masks.append((blk[:, None] == blk[None, :]) & (half[:, None] == 1) & (half[None, :] == 0))
        w //= 2
    masks.append(r[:, None] == r[None, :])
    wmat = np.concatenate(mats, axis=0).astype(np.float32)
    return jnp.asarray(wmat, BF), jnp.asarray(np.stack(masks).astype(np.float32))


def _hgrn_kernel(q_ref, f_ref, i_ref, g_ref, lb_ref, hg_ref, w_ref, mk_ref, bd_ref, s0_ref,
                 o_ref, st_ref, *, groups, chunk, nlev):
    c = chunk

    @pl.when(pl.program_id(1) == 0)
    def _():
        st_ref[...] = s0_ref[...]

    lbb = jnp.clip(lb_ref[...], 0.0, 1.0)
    wmat = w_ref[...]
    for g in range(groups):
        a = f_ref[g]
        q = q_ref[g]
        iv = i_ref[g]
        f = lbb + (1.0 - lbb) * _sigmoid(a)
        lf = jnp.log(jnp.maximum(f, F_FLOOR))
        kk = (1.0 - lbb) * _sigmoid(-a)
        e = _mm_exact_lhs(wmat, lf)
        b = e[0:c]
        qb16 = (q * jnp.exp(b)).astype(BF)
        kr16 = (kk * jnp.exp(e[c:2 * c])).astype(BF)
        dec_last = jnp.exp(b[c - 1:c])
        iv16 = iv.astype(BF)
        qt = [(q * jnp.exp(e[(2 + l) * c:(3 + l) * c])).astype(BF) for l in range(nlev)]
        kt = [(kk * jnp.exp(e[(2 + l) * c:(3 + l) * c])).astype(BF) for l in range(nlev)]
        qt.append(q.astype(BF))
        kt.append(kk.astype(BF))
        st = st_ref[g]
        st16 = st.astype(BF)
        o_heads, st_heads = [], []
        for h in range(NH):
            sl = slice(h * HEAD_DIM, (h + 1) * HEAD_DIM)
            att = jnp.zeros((c, c), F32)
            for l in range(nlev + 1):
                att = att + mk_ref[l] * _mm_nt(qt[l][:, sl], kt[l][:, sl])
            o_heads.append(_mm(att.astype(BF), iv16[:, sl]) + _mm_nt(qb16[:, sl], st16[:, sl]))
            st_heads.append(st[:, sl] * dec_last[:, sl] + _mm_tn(iv16[:, sl], kr16[:, sl]))
        o = jnp.concatenate(o_heads, axis=1)
        st_ref[g] = jnp.concatenate(st_heads, axis=1)
        oh, ol = _split2(o * o)
        ms = _mm(oh, bd_ref[...]) + _mm(ol, bd_ref[...])
        o_ref[g] = o * lax.rsqrt(ms + EPS) * hg_ref[...] * _silu(g_ref[g])


def _hgrn(u3, lb, hg, s0t, chunk, groups):
    bsz, t, _ = u3.shape
    nlev = int(math.log2(chunk))
    wmat, masks = _hgrn_consts(chunk)
    bd = jnp.asarray(np.kron(np.eye(NH), np.full((HEAD_DIM, HEAD_DIM), 1.0 / HEAD_DIM)), BF)
    col = lambda k: pl.BlockSpec((groups, chunk, GW), lambda b, c: (b, c, k))
    const = lambda shape: pl.BlockSpec(shape, lambda b, c: (0,) * len(shape))
    return pl.pallas_call(
        functools.partial(_hgrn_kernel, groups=groups, chunk=chunk, nlev=nlev),
        out_shape=(jax.ShapeDtypeStruct((bsz, t, GW), F32),
                   jax.ShapeDtypeStruct((bsz, HEAD_DIM, GW), F32)),
        grid=(bsz // groups, t // chunk),
        in_specs=[col(0), col(1), col(2), col(3), const((1, GW)), const((1, GW)),
                  const(wmat.shape), const(masks.shape), const((GW, GW)),
                  pl.BlockSpec((groups, HEAD_DIM, GW), lambda b, c: (b, 0, 0))],
        out_specs=(pl.BlockSpec((groups, chunk, GW), lambda b, c: (b, c, 0)),
                   pl.BlockSpec((groups, HEAD_DIM, GW), lambda b, c: (b, 0, 0))),
        compiler_params=_params(),
        name="hgrn2",
    )(u3, u3, u3, u3, lb, hg, wmat, masks, bd, s0t)


def _rope_tables(pos):
    half = ROT_DIM // 2
    inv = np.exp(-math.log(ROPE_THETA) * np.arange(half, dtype=np.float32) * 2.0 / ROT_DIM)
    ang = jnp.asarray(pos, F32)[:, None] * jnp.asarray(inv)[None, :]
    cos, sin = jnp.cos(ang), jnp.sin(ang)
    n = ang.shape[0]
    ones = jnp.ones((n, HEAD_DIM - ROT_DIM), F32)
    zeros = jnp.zeros((n, HEAD_DIM - ROT_DIM), F32)
    ch = jnp.concatenate([cos, cos, ones], axis=1)
    sh = jnp.concatenate([-sin, sin, zeros], axis=1)
    return jnp.tile(ch, (1, NH)), jnp.tile(sh, (1, NH))


def _rope_kernel(q_ref, k_ref, cos_ref, sin_ref, qo_ref, ko_ref, k16_ref, km_ref):
    cos = cos_ref[...]
    sin = sin_ref[...]
    lane = lax.broadcasted_iota(jnp.int32, cos.shape, 1) % HEAD_DIM
    first = lane < ROT_DIM // 2

    def rot(x):
        partner = jnp.where(first, pltpu.roll(x, GW - ROT_DIM // 2, 1), pltpu.roll(x, ROT_DIM // 2, 1))
        return x * cos + partner * sin

    qo_ref[...] = rot(q_ref[...])
    k = rot(k_ref[...])
    ko_ref[...] = k
    k16_ref[...] = k.astype(BF)
    km_ref[0] = jnp.mean(k, axis=0, keepdims=True)


def _rope(u, cos, sin, nbatch, rows):
    n = u.shape[0]
    t = n // nbatch
    nt = t // rows
    return pl.pallas_call(
        _rope_kernel,
        out_shape=(jax.ShapeDtypeStruct((n, GW), F32), jax.ShapeDtypeStruct((n, GW), F32),
                   jax.ShapeDtypeStruct((n, GW), BF), jax.ShapeDtypeStruct((n // rows, 1, GW), F32)),
        grid=(nbatch, nt),
        in_specs=[pl.BlockSpec((rows, GW), lambda b, j: (b * nt + j, 4)),
                  pl.BlockSpec((rows, GW), lambda b, j: (b * nt + j, 5)),
                  pl.BlockSpec((rows, GW), lambda b, j: (j, 0)),
                  pl.BlockSpec((rows, GW), lambda b, j: (j, 0))],
        out_specs=(pl.BlockSpec((rows, GW), lambda b, j: (b * nt + j, 0)),
                   pl.BlockSpec((rows, GW), lambda b, j: (b * nt + j, 0)),
                   pl.BlockSpec((rows, GW), lambda b, j: (b * nt + j, 0)),
                   pl.BlockSpec((1, 1, GW), lambda b, j: (b * nt + j, 0, 0))),
        compiler_params=_params(),
        name="rope",
    )(u, u, cos, sin)


def _topk_rows(scores, valid, idx):
    cur = jnp.where(valid, scores, NEG_BIG)
    picked = jnp.zeros(scores.shape, F32)
    big = jnp.int32(1 << 30)
    for _ in range(MOBA_TOPK):
        mx = jnp.max(cur, axis=0, keepdims=True)
        first = jnp.min(jnp.where(cur == mx, idx, big), axis=0, keepdims=True)
        pick = idx == first
        picked = jnp.where(pick & valid, 1.0, picked)
        cur = jnp.where(pick, -jnp.inf, cur)
    return picked


def _pair_weights(qt16, tq):
    z = jnp.zeros((HEAD_DIM, tq), qt16.dtype)
    out = []
    for p in range(NH // 2):
        a = qt16[(2 * p) * HEAD_DIM:(2 * p + 1) * HEAD_DIM]
        b = qt16[(2 * p + 1) * HEAD_DIM:(2 * p + 2) * HEAD_DIM]
        out.append(jnp.concatenate([jnp.concatenate([a, z], axis=1), jnp.concatenate([z, b], axis=1)], axis=0))
    return out


def _moba_kernel(q_ref, k_ref, vt_ref, km_ref, o_ref, m_ref, l_ref, acc_ref, sel_ref, *, nb):
    own = pl.program_id(1)
    q = q_ref[...]
    tq = q.shape[0]
    qt = jnp.transpose(q)
    rowi = lax.broadcasted_iota(jnp.int32, (nb, tq), 0)
    km = km_ref[...]
    for h in range(NH):
        sl = slice(h * HEAD_DIM, (h + 1) * HEAD_DIM)
        sel_ref[h] = _topk_rows(_mm3(km[:, sl], qt[sl, :]), rowi < own, rowi)
    qp = _pair_weights((qt * SCALE).astype(BF), tq)
    m_ref[...] = jnp.full(m_ref.shape, -jnp.inf, F32)
    l_ref[...] = jnp.zeros(l_ref.shape, F32)
    acc_ref[...] = jnp.zeros(acc_ref.shape, F32)

    def block(start, keep_fn):
        kblk = k_ref[pl.ds(start, MOBA_BLOCK), :]
        for p in range(NH // 2):
            sp = _mm(kblk[:, p * 2 * HEAD_DIM:(p + 1) * 2 * HEAD_DIM], qp[p])
            for r in range(2):
                h = 2 * p + r
                rows = slice(h * HEAD_DIM, (h + 1) * HEAD_DIM)
                s = jnp.where(keep_fn(h), sp[:, r * tq:(r + 1) * tq], NEG_BIG)
                m_old = m_ref[h:h + 1, :]
                m_new = jnp.maximum(m_old, jnp.max(s, axis=0, keepdims=True))
                pt = jnp.exp(s - m_new)
                alpha = jnp.exp(m_old - m_new)
                l_ref[h:h + 1, :] = alpha * l_ref[h:h + 1, :] + jnp.sum(pt, axis=0, keepdims=True)
                vt = vt_ref[rows, pl.ds(start, MOBA_BLOCK)]
                acc_ref[rows, :] = alpha * acc_ref[rows, :] + _mm(vt, pt.astype(BF))
                m_ref[h:h + 1, :] = m_new

    def past(n, carry):
        block(pl.multiple_of(n * MOBA_BLOCK, MOBA_BLOCK), lambda h: sel_ref[h, pl.ds(n, 1), :] > 0.5)
        return carry

    lax.fori_loop(0, own, past, 0)

    keyi = lax.broadcasted_iota(jnp.int32, (MOBA_BLOCK, tq), 0)
    qi = lax.broadcasted_iota(jnp.int32, (MOBA_BLOCK, tq), 1)
    causal = keyi <= qi
    block(pl.multiple_of(own * MOBA_BLOCK, MOBA_BLOCK), lambda h: causal)
    out_t = jnp.concatenate(
        [acc_ref[h * HEAD_DIM:(h + 1) * HEAD_DIM, :] / l_ref[h:h + 1, :] for h in range(NH)], axis=0)
    o_ref[...] = jnp.transpose(out_t)


def _moba_prompt(qrot, k16, vt16, kmean, nbatch):
    n = qrot.shape[0]
    t = n // nbatch
    nq = t // MOBA_BLOCK
    nb = t // MOBA_BLOCK
    return pl.pallas_call(
        functools.partial(_moba_kernel, nb=nb),
        out_shape=jax.ShapeDtypeStruct((n, GW), F32),
        grid=(nbatch, nq),
        in_specs=[pl.BlockSpec((MOBA_BLOCK, GW), lambda b, j: (b * nq + j, 0)),
                  pl.BlockSpec((t, GW), lambda b, j: (b, 0)),
                  pl.BlockSpec((GW, t), lambda b, j: (b, 0)),
                  pl.BlockSpec((nb, GW), lambda b, j: (b, 0))],
        out_specs=pl.BlockSpec((MOBA_BLOCK, GW), lambda b, j: (b * nq + j, 0)),
        scratch_shapes=[pltpu.VMEM((8, MOBA_BLOCK), F32), pltpu.VMEM((8, MOBA_BLOCK), F32),
                        pltpu.VMEM((GW, MOBA_BLOCK), F32), pltpu.VMEM((NH, nb, MOBA_BLOCK), F32)],
        compiler_params=_params(),
        name="moba_prompt",
    )(qrot, k16, vt16, kmean)


SB_BLOCK = 256
SB_CUTOFF = 104.0


def _later_mask(n, rows_are_keys):
    r = np.arange(n)
    m = r[None, :] > r[:, None]
    return jnp.asarray((m if rows_are_keys else m.T).astype(np.float32), BF)


def _sb_kernel(q_ref, k_ref, vt_ref, up_ref, o_ref, carry_ref, acc_ref):
    j = pl.program_id(1)
    tq = q_ref.shape[0]
    qp = _pair_weights((jnp.transpose(q_ref[...]) * SCALE).astype(BF), tq)
    lower = up_ref[...]
    carry_ref[...] = jnp.zeros(carry_ref.shape, F32)
    acc_ref[...] = jnp.zeros(acc_ref.shape, F32)

    def block(start, mask):
        kblk = k_ref[pl.ds(start, SB_BLOCK), :]
        for p in range(NH // 2):
            zp = _mm(kblk[:, p * 2 * HEAD_DIM:(p + 1) * 2 * HEAD_DIM], qp[p])
            for r in range(2):
                h = 2 * p + r
                rows = slice(h * HEAD_DIM, (h + 1) * HEAD_DIM)
                lpos, lneg = _log_sigmoid_pair(zp[:, r * tq:(r + 1) * tq])
                if mask is not None:
                    lneg = jnp.where(mask, lneg, 0.0)
                hi, lo = _split2(lneg)
                rest = carry_ref[h:h + 1, :] + _mm(lower, hi) + _mm(lower, lo)
                a = jnp.exp(lpos + jnp.minimum(rest, 0.0))
                if mask is not None:
                    a = jnp.where(mask, a, 0.0)
                vt = vt_ref[rows, pl.ds(start, SB_BLOCK)]
                acc_ref[rows, :] = acc_ref[rows, :] + _mm(vt, a.astype(BF))
                carry_ref[h:h + 1, :] = carry_ref[h:h + 1, :] + jnp.sum(lneg, axis=0, keepdims=True)

    def active():
        return (jnp.max(carry_ref[0:NH, :]) > -SB_CUTOFF).astype(jnp.int32)

    keyi = lax.broadcasted_iota(jnp.int32, (SB_BLOCK, tq), 0)
    qi = lax.broadcasted_iota(jnp.int32, (SB_BLOCK, tq), 1)
    block(pl.multiple_of(j * SB_BLOCK, SB_BLOCK), keyi < qi)

    def body(c):
        block(pl.multiple_of(c[0] * SB_BLOCK, SB_BLOCK), None)
        return c[0] - 1, active()

    lax.while_loop(lambda c: (c[0] >= 0) & (c[1] > 0), body, (j - 1, active()))
    o_ref[...] = jnp.transpose(acc_ref[...])


def _sb_prompt(u, u16, vt16, nbatch):
    n = u.shape[0]
    t = n // nbatch
    nq = t // SB_BLOCK
    return pl.pallas_call(
        _sb_kernel,
        out_shape=jax.ShapeDtypeStruct((n, GW), F32),
        grid=(nbatch, nq),
        in_specs=[pl.BlockSpec((SB_BLOCK, GW), lambda b, j: (b * nq + j, 7)),
                  pl.BlockSpec((t, GW), lambda b, j: (b, 2)),
                  pl.BlockSpec((GW, t), lambda b, j: (b, 0)),
                  pl.BlockSpec((SB_BLOCK, SB_BLOCK), lambda b, j: (0, 0))],
        out_specs=pl.BlockSpec((SB_BLOCK, GW), lambda b, j: (b * nq + j, 0)),
        scratch_shapes=[pltpu.VMEM((8, SB_BLOCK), F32), pltpu.VMEM((GW, SB_BLOCK), F32)],
        compiler_params=_params(),
        name="sb_prompt",
    )(u, u16, vt16, _later_mask(SB_BLOCK, True))


def _block_diag_q(q, nbatch, t):
    q4 = q.reshape(nbatch, t, NH, HEAD_DIM)
    eye = jnp.eye(NH, dtype=q.dtype)
    bd = q4[:, None, :, :, :] * eye[None, :, None, :, None]
    return bd.reshape(nbatch, NH * t, GW)


def _page_specs(layer, n_phys, n_pages):
    def spec(p):
        return pl.BlockSpec((1, GW, PAGE_SIZE), lambda b, pt: (layer * n_phys + pt[b * n_pages + p], 0, 0))
    return [spec(p) for p in range(n_pages)]


def _heads_out(acc, den, t):
    return jnp.concatenate(
        [acc[h * t:(h + 1) * t, h * HEAD_DIM:(h + 1) * HEAD_DIM] / den[h * t:(h + 1) * t]
         for h in range(NH)], axis=1)


def _pad_rows(x, rows):
    return jnp.concatenate([x, jnp.zeros((rows - x.shape[0], x.shape[1]), x.dtype)], axis=0)


def _moba_dec_kernel(pt_ref, qbd_ref, knew_ref, vnew_ref, *refs, n_pages, t, own):
    del pt_ref
    kt_refs, vt_refs, o_ref = refs[:n_pages], refs[n_pages:2 * n_pages], refs[2 * n_pages]
    qbd = qbd_ref[0]
    nrow = qbd.shape[0]
    ppb = MOBA_BLOCK // PAGE_SIZE
    lg = [_mm3(qbd, kt_refs[p][0]) for p in range(n_pages)]
    score = []
    for n in range(own):
        tot = jnp.sum(lg[n * ppb], axis=1, keepdims=True)
        for r in range(1, ppb):
            tot = tot + jnp.sum(lg[n * ppb + r], axis=1, keepdims=True)
        score.append(tot * (1.0 / MOBA_BLOCK))
    sel = []
    for n in range(own):
        rank = jnp.zeros((nrow, 1), F32)
        for m in range(own):
            if m != n:
                beats = (score[m] >= score[n]) if m < n else (score[m] > score[n])
                rank = rank + jnp.where(beats, 1.0, 0.0)
        sel.append(rank < float(MOBA_TOPK))
    logits = [jnp.where(sel[p // ppb], lg[p] * SCALE, NEG_BIG) for p in range(n_pages)]
    lnew = _mm3_nt(qbd, _pad_rows(knew_ref[...], PAGE_SIZE)) * SCALE
    keyi = lax.broadcasted_iota(jnp.int32, lnew.shape, 1)
    qi = lax.broadcasted_iota(jnp.int32, lnew.shape, 0) % t
    lnew = jnp.where(keyi <= qi, lnew, NEG_BIG)
    m = jnp.max(lnew, axis=1, keepdims=True)
    for lp in logits:
        m = jnp.maximum(m, jnp.max(lp, axis=1, keepdims=True))
    enew = jnp.exp(lnew - m)
    den = jnp.sum(enew, axis=1, keepdims=True)
    acc = _mm(enew.astype(BF), _pad_rows(vnew_ref[...], PAGE_SIZE).astype(BF))
    for p in range(n_pages):
        e = jnp.exp(logits[p] - m)
        den = den + jnp.sum(e, axis=1, keepdims=True)
        acc = acc + _mm_nt(e.astype(BF), vt_refs[p][0].astype(BF))
    o_ref[...] = _heads_out(acc, den, t)


def _sb_dec_kernel(pt_ref, qbd_ref, knew_ref, vnew_ref, later_ref, *refs, n_pages, t):
    del pt_ref
    kt_refs, vt_refs, o_ref = refs[:n_pages], refs[n_pages:2 * n_pages], refs[2 * n_pages]
    qbd16 = (qbd_ref[0] * SCALE).astype(BF)
    nrow = qbd16.shape[0]
    later = later_ref[...]

    def block(z, carry, mask):
        lpos, lneg = _log_sigmoid_pair(z)
        if mask is not None:
            lneg = jnp.where(mask, lneg, 0.0)
        hi, lo = _split2(lneg)
        rest = carry + _mm(hi, later) + _mm(lo, later)
        a = jnp.exp(lpos + jnp.minimum(rest, 0.0))
        if mask is not None:
            a = jnp.where(mask, a, 0.0)
        return a.astype(BF), carry + jnp.sum(lneg, axis=1, keepdims=True)

    keyi = lax.broadcasted_iota(jnp.int32, (nrow, PAGE_SIZE), 1)
    qi = lax.broadcasted_iota(jnp.int32, (nrow, PAGE_SIZE), 0) % t
    znew = _mm_nt(qbd16, _pad_rows(knew_ref[...], PAGE_SIZE).astype(BF))
    a, carry = block(znew, jnp.zeros((nrow, 1), F32), keyi < qi)
    acc = _mm(a, _pad_rows(vnew_ref[...], PAGE_SIZE).astype(BF))
    for p in range(n_pages - 1, -1, -1):
        a, carry = block(_mm(qbd16, kt_refs[p][0].astype(BF)), carry, None)
        acc = acc + _mm_nt(a, vt_refs[p][0].astype(BF))
    o_ref[...] = _heads_out(acc, jnp.ones((nrow, 1), F32), t)


def _dec_attention(kind, layer, page_table, qbd, knew, vnew, kcache, vcache, t, pos0):
    nbatch, n_pages = page_table.shape
    n_phys = kcache.shape[0] // DEPTH
    assert pos0 == n_pages * PAGE_SIZE and pos0 % MOBA_BLOCK == 0 and t <= PAGE_SIZE
    pspecs = _page_specs(layer, n_phys, n_pages)
    row = pl.BlockSpec((t, GW), lambda b, pt: (b, 0))
    in_specs = [pl.BlockSpec((1, NH * t, GW), lambda b, pt: (b, 0, 0)), row, row]
    args = [qbd, knew, vnew]
    if kind == "moba":
        body = functools.partial(_moba_dec_kernel, n_pages=n_pages, t=t, own=pos0 // MOBA_BLOCK)
    else:
        body = functools.partial(_sb_dec_kernel, n_pages=n_pages, t=t)
        in_specs.append(pl.BlockSpec((PAGE_SIZE, PAGE_SIZE), lambda b, pt: (0, 0)))
        args.append(_later_mask(PAGE_SIZE, False))
    in_specs += pspecs + pspecs
    args += [kcache] * n_pages + [vcache] * n_pages
    return pl.pallas_call(
        body,
        out_shape=jax.ShapeDtypeStruct((nbatch * t, GW), F32),
        grid_spec=pltpu.PrefetchScalarGridSpec(
            num_scalar_prefetch=1, grid=(nbatch,), in_specs=in_specs,
            out_specs=pl.BlockSpec((t, GW), lambda b, pt: (b, 0))),
        compiler_params=_params(),
        name=kind + "_decode",
    )(page_table.reshape(-1), *args)


POOL_HALO = 16


def _pool_bands(seqs, t):
    halo_seq = np.repeat(np.arange(seqs), POOL_HALO)
    halo_e = np.tile(np.arange(POOL_HALO), seqs)
    row_seq = np.repeat(np.arange(seqs), t)
    row_e = np.tile(np.arange(t), seqs) + POOL_HALO
    col_seq = np.concatenate([halo_seq, row_seq])[None, :]
    col_e = np.concatenate([halo_e, row_e])[None, :]
    same = col_seq == row_seq[:, None]
    r = row_e[:, None]
    return jnp.asarray(np.stack([(same & (col_e <= r) & (col_e > r - w)).astype(np.float32)
                                 for w in POOL_WINDOWS]), BF)


def _pool_kernel(u_ref, halo_ref, buf_ref, band_ref, pw_ref, ps_ref, o_ref, *, rows, nt, t_seq, pos0):
    i = pl.program_id(0)
    u = u_ref[...]
    halo = jnp.where(i % nt == 0, buf_ref[...], halo_ref[...])
    ext = jnp.concatenate([halo, u], axis=0)
    hi, mid, lo = _split3(ext)
    tok = (i * rows + lax.broadcasted_iota(jnp.int32, (rows, 1), 0)) % t_seq
    pos = (pos0 + tok).astype(F32)
    parts = []
    gc = GW // len(POOL_WINDOWS)
    for gi, w in enumerate(POOL_WINDOWS):
        sl = slice(gi * gc, (gi + 1) * gc)
        band = band_ref[gi]
        ssum = _mm(band, hi[:, sl]) + _mm(band, mid[:, sl]) + _mm(band, lo[:, sl])
        cnt = jnp.minimum(float(w), pos + 1.0)
        parts.append(ssum / cnt - u[:, sl])
    d = jnp.concatenate(parts, axis=1)
    o_ref[...] = _mm(d.astype(BF), pw_ref[...]) * ps_ref[...]


def _pool(u, buf_flat, pw_bd16, ps, nbatch, t, pos0):
    n = u.shape[0]
    if t >= MOBA_BLOCK:
        rows, seqs, hr = 256, 1, POOL_HALO
        nt = t // rows
        band = _pool_bands(1, rows)
        halo_arr = u
        halo_spec = pl.BlockSpec((hr, GW), lambda i: (jnp.maximum(i * (rows // hr) - 1, 0), 10))
        buf_spec = pl.BlockSpec((hr, GW), lambda i: (i // nt, 0))
    else:
        seqs = 128 // t
        rows, hr, nt = seqs * t, seqs * POOL_HALO, 1
        band = _pool_bands(seqs, t)
        halo_arr = buf_flat
        halo_spec = pl.BlockSpec((hr, GW), lambda i: (i, 0))
        buf_spec = halo_spec
    return pl.pallas_call(
        functools.partial(_pool_kernel, rows=rows, nt=nt, t_seq=t, pos0=pos0),
        out_shape=jax.ShapeDtypeStruct((n, GW), F32),
        grid=(n // rows,),
        in_specs=[pl.BlockSpec((rows, GW), lambda i: (i, 10)), halo_spec, buf_spec,
                  pl.BlockSpec(band.shape, lambda i: (0, 0, 0)),
                  pl.BlockSpec((GW, GW), lambda i: (0, 0)),
                  pl.BlockSpec((1, GW), lambda i: (0, 0))],
        out_specs=pl.BlockSpec((rows, GW), lambda i: (i, 0)),
        compiler_params=_params(),
        name="pool_mixer",
    )(u, halo_arr, buf_flat, band, pw_bd16, ps)


def _route(logits):
    lane = lax.broadcasted_iota(jnp.int32, logits.shape, 1)
    big = jnp.int32(1 << 30)
    gmask = lane < N_GROUPS
    gl = jnp.where(gmask, logits, -jnp.inf)
    gmax = jnp.max(gl, axis=-1, keepdims=True)
    gsum = jnp.sum(jnp.where(gmask, jnp.exp(gl - gmax), 0.0), axis=-1, keepdims=True)
    g_top = 1.0 / gsum
    g_idx = jnp.min(jnp.where(gl == gmax, lane, big), axis=-1, keepdims=True)
    emask = (lane >= N_GROUPS) & (lane < N_GROUPS + N_EXPERTS) & ((lane - N_GROUPS) // EXP_PER_GROUP == g_idx)
    el = jnp.where(emask, logits, -jnp.inf)
    m1 = jnp.max(el, axis=-1, keepdims=True)
    i1 = jnp.min(jnp.where(el == m1, lane, big), axis=-1, keepdims=True)
    el2 = jnp.where(lane == i1, -jnp.inf, el)
    m2 = jnp.max(el2, axis=-1, keepdims=True)
    i2 = jnp.min(jnp.where(el2 == m2, lane, big), axis=-1, keepdims=True)
    r = jnp.exp(m2 - m1)
    w1 = g_top / (1.0 + r)
    w2 = g_top * r / (1.0 + r)
    return jnp.where(lane == i1, w1, jnp.where(lane == i2, w2, 0.0))


def _outproj_kernel(oa_ref, ob_ref, oc_ref, od_ref, x_ref, gt_ref, g_ref, sc_ref, sh_ref, w_ref,
                    wr_ref, br_ref, x1_ref, h16_ref, gate_ref):
    merged = jnp.concatenate([oa_ref[...], ob_ref[...], oc_ref[...], od_ref[...]], axis=1).astype(BF)
    x1 = x_ref[...] + _mod_val(gt_ref) * _mm(merged, w_ref[...])
    x1_ref[...] = x1
    h = _rms_mod(x1, g_ref[...], _mod_val(sc_ref), _mod_val(sh_ref))
    h16_ref[...] = h.astype(BF)
    gate_ref[...] = _route(_mm3(h, wr_ref[...]) + br_ref[...])


def _outproj(oa, ob, oc, od, x, gt, g, sc, sh, w16, wr, br, rows_per_group, tm):
    n = x.shape[0]
    quarter = pl.BlockSpec((tm, GW), lambda i: (i, 0))
    full = pl.BlockSpec((tm, D_MODEL), lambda i: (i, 0))
    return pl.pallas_call(
        _outproj_kernel,
        out_shape=(jax.ShapeDtypeStruct((n, D_MODEL), F32), jax.ShapeDtypeStruct((n, D_MODEL), BF),
                   jax.ShapeDtypeStruct((n, LANES), F32)),
        grid=(n // tm,),
        in_specs=[quarter, quarter, quarter, quarter, full,
                  _mod_spec(gt, tm, rows_per_group),
                  pl.BlockSpec((1, D_MODEL), lambda i: (0, 0)),
                  _mod_spec(sc, tm, rows_per_group), _mod_spec(sh, tm, rows_per_group),
                  pl.BlockSpec((D_MODEL, D_MODEL), lambda i: (0, 0)),
                  pl.BlockSpec((D_MODEL, LANES), lambda i: (0, 0)),
                  pl.BlockSpec((1, LANES), lambda i: (0, 0))],
        out_specs=(full, full, pl.BlockSpec((tm, LANES), lambda i: (i, 0))),
        compiler_params=_params(),
        name="out_proj_router",
    )(oa, ob, oc, od, x, gt, g, sc, sh, w16, wr, br)


def _moe_kernel(h_ref, gate_ref, x1_ref, gt_ref, wg_ref, wu_ref, wd_ref, o_ref, acc_ref):
    e = pl.program_id(1)

    @pl.when(e == 0)
    def _():
        acc_ref[...] = jnp.zeros(acc_ref.shape, F32)

    h = h_ref[...]
    a = _silu(_mm(h, wg_ref[0])) * _mm(h, wu_ref[0])
    lane = lax.broadcasted_iota(jnp.int32, gate_ref.shape, 1)
    gcol = jnp.sum(jnp.where(lane == e + N_GROUPS, gate_ref[...], 0.0), axis=-1, keepdims=True)
    acc_ref[...] += gcol * _mm(a.astype(BF), wd_ref[0])

    @pl.when(e == N_EXPERTS - 1)
    def _():
        o_ref[...] = x1_ref[...] + _mod_val(gt_ref) * acc_ref[...]


def _moe(h16, gate, x1, gt, wg16, wu16, wd16, layer, rows_per_group, tm):
    n = x1.shape[0]
    full = lambda: pl.BlockSpec((tm, D_MODEL), lambda i, e: (i, 0))
    if gt.ndim == 3:
        tiles = rows_per_group // tm
        gt_spec = pl.BlockSpec((1, 1, D_MODEL), lambda i, e: (i // tiles, 0, 0))
    else:
        gt_spec = pl.BlockSpec((tm, D_MODEL), lambda i, e: (i, 0))
    return pl.pallas_call(
        _moe_kernel,
        out_shape=jax.ShapeDtypeStruct((n, D_MODEL), F32),
        grid=(n // tm, N_EXPERTS),
        in_specs=[full(), pl.BlockSpec((tm, LANES), lambda i, e: (i, 0)), full(), gt_spec,
                  pl.BlockSpec((1, D_MODEL, D_EXPERT), lambda i, e: (layer * N_EXPERTS + e, 0, 0)),
                  pl.BlockSpec((1, D_MODEL, D_EXPERT), lambda i, e: (layer * N_EXPERTS + e, 0, 0)),
                  pl.BlockSpec((1, D_EXPERT, D_MODEL), lambda i, e: (layer * N_EXPERTS + e, 0, 0))],
        out_specs=full(),
        scratch_shapes=[pltpu.VMEM((tm, D_MODEL), F32)],
        compiler_params=_params(),
        name="moe_experts",
    )(h16, gate, x1, gt, wg16, wu16, wd16)


def _final_kernel(x_ref, g_ref, o_ref):
    x = x_ref[...]
    ms = jnp.mean(x * x, axis=-1, keepdims=True)
    o_ref[...] = x * lax.rsqrt(ms + EPS) * g_ref[...]


def _final_norm(x, g, tm):
    n = x.shape[0]
    return pl.pallas_call(
        _final_kernel,
        out_shape=jax.ShapeDtypeStruct((n, D_MODEL), F32),
        grid=(n // tm,),
        in_specs=[pl.BlockSpec((tm, D_MODEL), lambda i: (i, 0)), pl.BlockSpec((1, D_MODEL), lambda i: (0, 0))],
        out_specs=pl.BlockSpec((tm, D_MODEL), lambda i: (i, 0)),
        compiler_params=_params(),
        name="final_norm",
    )(x, g)


def _state_in(s):
    b = s.shape[0]
    return s.transpose(0, 3, 1, 2).reshape(b, HEAD_DIM, GW)


def _state_out(s):
    b = s.shape[0]
    return s.reshape(b, HEAD_DIM, NH, HEAD_DIM).transpose(0, 2, 3, 1)


def _hgrn_lower_bounds(lb_logits):
    p = jax.nn.softmax(lb_logits.astype(F32), axis=0)
    return jnp.cumsum(p, axis=0) - p[0:1]


def _layer(x, mods, wl, layer, nbatch, t, pos0, s0, pbuf, past, tm):
    sh1, sc1, gt1, sh2, sc2, gt2 = mods
    n = nbatch * t
    u, u16 = _inproj(x, wl["g_mix"], sc1, sh1, wl["w_in"], t, tm)
    u3 = u.reshape(nbatch, t, IN_COLS)

    chunk = HG_CHUNK if t % HG_CHUNK == 0 else t
    groups = 2 if t % HG_CHUNK == 0 else 8
    oa, st = _hgrn(u3, wl["lb"], wl["hg_g"], _state_in(s0), chunk, groups)

    if past is None:
        cos, sin = _rope_tables(np.arange(t) + pos0)
        qrot, krot, krot16, kmean = _rope(u, cos, sin, nbatch, MOBA_BLOCK)
        u16t = u16.reshape(nbatch, t, 4 * GW).transpose(0, 2, 1)
        vtb = u16t[:, 0:GW].reshape(nbatch * GW, t)
        vtc = u16t[:, 3 * GW:4 * GW].reshape(nbatch * GW, t)
        ob = _moba_prompt(qrot, krot16, vtb, kmean.reshape(-1, GW), nbatch)
        oc = _sb_prompt(u, u16, vtc, nbatch)
    else:
        cos, sin = _rope_tables(np.tile(np.arange(t), nbatch) + pos0)
        qrot, krot, _, _ = _rope(u, cos, sin, 1, n)
        page_table, cmk, cmv, csk, csv = past
        vb = u[:, 6 * GW:7 * GW]
        ob = _dec_attention("moba", layer, page_table, _block_diag_q(qrot, nbatch, t), krot, vb, cmk, cmv, t, pos0)
        qc, kc, vc = u[:, 7 * GW:8 * GW], u[:, 8 * GW:9 * GW], u[:, 9 * GW:10 * GW]
        oc = _dec_attention("sb", layer, page_table, _block_diag_q(qc, nbatch, t), kc, vc, csk, csv, t, pos0)

    buf_flat = jnp.pad(pbuf, ((0, 0), (POOL_HALO - POOL_BUF, 0), (0, 0))).reshape(nbatch * POOL_HALO, GW)
    od = _pool(u, buf_flat, wl["pool_w"], wl["pool_scale"], nbatch, t, pos0)
    ud = u3[:, :, 10 * GW:]
    buf_new = jnp.concatenate([pbuf, ud], axis=1)[:, -POOL_BUF:]

    x1, h16, gate = _outproj(oa.reshape(n, GW), ob, oc, od, x, gt1, wl["g_ffn"], sc2, sh2,
                             wl["w_out"], wl["wr"], wl["br"], t, tm)
    x2 = _moe(h16, gate, x1, gt2, wl["ewg"], wl["ewu"], wl["ewd"], layer, t, tm)

    rows = lambda z: z.reshape(nbatch, t, NH, HEAD_DIM)
    states = (rows(krot), rows(u[:, 6 * GW:7 * GW]), rows(u[:, 8 * GW:9 * GW]), rows(u[:, 9 * GW:10 * GW]),
              _state_out(st), buf_new)
    return x2, states


def kernel(x_prompt, x_sample, cache_moba_k, cache_moba_v, cache_sb_k, cache_sb_v, state_hgrn, state_pool,
           page_table, c_prompt, c_sample, ada_w, ada_b, norm_mix_g, norm_ffn_g, w_in, hgrn_lb, hgrn_norm_g,
           pool_w, pool_scale, w_out, router_group_w, router_group_b, router_expert_w, router_expert_b,
           expert_w_gate, expert_w_up, expert_w_down, final_norm_g):
    bp, tp, _ = x_prompt.shape
    bs, ts, _ = x_sample.shape
    pos0_s = page_table.shape[1] * PAGE_SIZE
    n_phys = cache_moba_k.shape[1]

    c_all = jnp.concatenate([c_prompt, c_sample], axis=0)
    mpad = (-c_all.shape[0]) % 8
    mod = _ada_all(jnp.pad(c_all, ((0, mpad), (0, 0))), ada_w, ada_b)

    lbs = _hgrn_lower_bounds(hgrn_lb)
    w_in16 = w_in.astype(BF)
    w_out16 = w_out.astype(BF)
    ewg16 = expert_w_gate.astype(BF).reshape(DEPTH * N_EXPERTS, D_MODEL, D_EXPERT)
    ewu16 = expert_w_up.astype(BF).reshape(DEPTH * N_EXPERTS, D_MODEL, D_EXPERT)
    ewd16 = expert_w_down.astype(BF).reshape(DEPTH * N_EXPERTS, D_EXPERT, D_MODEL)
    wr = jnp.concatenate([router_group_w, router_expert_w], axis=-1)
    wr = jnp.pad(wr, ((0, 0), (0, 0), (0, LANES - wr.shape[-1])))
    br = jnp.concatenate([router_group_b, router_expert_b], axis=-1)
    br = jnp.pad(br, ((0, 0), (0, LANES - br.shape[-1])))
    caches = tuple(c.transpose(0, 1, 3, 4, 2).reshape(DEPTH * n_phys, GW, PAGE_SIZE)
                   for c in (cache_moba_k, cache_moba_v, cache_sb_k, cache_sb_v))

    xp = x_prompt.reshape(bp * tp, D_MODEL)
    xs = x_sample.reshape(bs * ts, D_MODEL)
    st_p, st_s = [], []
    for l in range(DEPTH):
        pw_bd = jnp.zeros((GW, GW), F32)
        gc = GW // len(POOL_WINDOWS)
        for gi in range(len(POOL_WINDOWS)):
            pw_bd = lax.dynamic_update_slice(pw_bd, pool_w[l, gi], (gi * gc, gi * gc))
        wl = dict(g_mix=norm_mix_g[l][None], g_ffn=norm_ffn_g[l][None], w_in=w_in16[l], lb=lbs[l][None],
                  hg_g=hgrn_norm_g[l][None], pool_w=pw_bd.astype(BF), pool_scale=pool_scale[l][None],
                  w_out=w_out16[l], wr=wr[l], br=br[l][None], ewg=ewg16, ewu=ewu16, ewd=ewd16)
        mp = mod[l, :bp].reshape(bp, 6, 1, D_MODEL)
        mods_p = tuple(mp[:, k] for k in range(6))
        ms = jnp.repeat(mod[l, bp:bp + bs], ts, axis=0).reshape(bs * ts, 6, D_MODEL)
        mods_s = tuple(ms[:, k] for k in range(6))
        xp, sp = _layer(xp, mods_p, wl, l, bp, tp, 0, jnp.zeros((bp, NH, HEAD_DIM, HEAD_DIM), F32),
                        jnp.zeros((bp, POOL_BUF, GW), F32), None, min(ROW_TILE, tp))
        xs, ss = _layer(xs, mods_s, wl, l, bs, ts, pos0_s, state_hgrn[l], state_pool[l],
                        (page_table,) + caches, min(ROW_TILE, bs * ts))
        st_p.append(sp)
        st_s.append(ss)

    y_prompt = _final_norm(xp, final_norm_g[None], min(ROW_TILE, tp)).reshape(bp, tp, D_MODEL)
    y_sample = _final_norm(xs, final_norm_g[None], min(ROW_TILE, bs * ts)).reshape(bs, ts, D_MODEL)
    stk = lambda sts, i: jnp.stack([s[i] for s in sts], axis=0)
    return (y_prompt, y_sample,
            stk(st_p, 0), stk(st_p, 1), stk(st_p, 2), stk(st_p, 3), stk(st_p, 4), stk(st_p, 5),
            stk(st_s, 0), stk(st_s, 1), stk(st_s, 2), stk(st_s, 3), stk(st_s, 4), stk(st_s, 5))
```

```python
import functools
import math

import numpy as np
import jax
import jax.numpy as jnp
from jax import lax
from jax.experimental import pallas as pl
from jax.experimental.pallas import tpu as pltpu

F32 = jnp.float32
BF = jnp.bfloat16

D_MODEL = 1024
DEPTH = 4
PAGE_SIZE = 128
HEAD_DIM = 64
GW = 256
NH = 4
HG_CHUNK = 64
HG_CHUNKS_PER_STEP = 4
MOBA_BLOCK = 256
MOBA_TOPK = 3
ROT_DIM = 16
ROPE_THETA = 500000.0
POOL_WINDOWS = (2, 4, 8, 16)
POOL_BUF = 15
N_GROUPS = 4
EXP_PER_GROUP = 4
N_EXPERTS = 16
D_EXPERT = 512
IN_COLS = 11 * GW
EPS = 1e-6
NEG_BIG = -1e30
F_FLOOR = 1e-30
SCALE = 1.0 / math.sqrt(HEAD_DIM)

V7X_VMEM_BYTES = 64 * 1024 * 1024
VMEM_LIMIT = V7X_VMEM_BYTES - 12 * 1024 * 1024
LANES = 128
ROW_TILE = 512


def _params():
    return pltpu.CompilerParams(vmem_limit_bytes=VMEM_LIMIT)


def _mm(a, b):
    return jnp.dot(a, b, preferred_element_type=F32)


def _mm_nt(a, b):
    return lax.dot_general(a, b, (((1,), (1,)), ((), ())), preferred_element_type=F32)


def _mm_tn(a, b):
    return lax.dot_general(a, b, (((0,), (0,)), ((), ())), preferred_element_type=F32)


def _split2(x):
    hi = x.astype(BF)
    lo = (x - hi.astype(F32)).astype(BF)
    return hi, lo


def _split3(x):
    hi = x.astype(BF)
    r = x - hi.astype(F32)
    mid = r.astype(BF)
    lo = (r - mid.astype(F32)).astype(BF)
    return hi, mid, lo


def _mm_exact_lhs(w01, x):
    hi, mid, lo = _split3(x)
    return _mm(w01, hi) + _mm(w01, mid) + _mm(w01, lo)


def _mm3_nt(a, b):
    ah, al = _split2(a)
    bh, bl = _split2(b)
    return _mm_nt(ah, bh) + _mm_nt(ah, bl) + _mm_nt(al, bh)


def _mm3(a, b):
    ah, al = _split2(a)
    bh, bl = _split2(b)
    return _mm(ah, bh) + _mm(ah, bl) + _mm(al, bh)


def _sigmoid(x):
    return 1.0 / (1.0 + jnp.exp(-x))


def _silu(x):
    return x * _sigmoid(x)


def _log_sigmoid_pair(z):
    t = jnp.log(1.0 + jnp.exp(-jnp.abs(z)))
    return jnp.minimum(z, 0.0) - t, -jnp.maximum(z, 0.0) - t


def _ada_kernel(c_ref, w_ref, b_ref, o_ref):
    o_ref[0] = _mm(c_ref[...].astype(BF), w_ref[0].astype(BF)) + b_ref[0]


def _ada_all(c_all, ada_w, ada_b):
    m = c_all.shape[0]
    tn = 1536
    return pl.pallas_call(
        _ada_kernel,
        out_shape=jax.ShapeDtypeStruct((DEPTH, m, 6 * D_MODEL), F32),
        grid=(DEPTH, 6 * D_MODEL // tn),
        in_specs=[
            pl.BlockSpec((m, D_MODEL), lambda l, j: (0, 0)),
            pl.BlockSpec((1, D_MODEL, tn), lambda l, j: (l, 0, j)),
            pl.BlockSpec((1, 1, tn), lambda l, j: (l, 0, j)),
        ],
        out_specs=pl.BlockSpec((1, m, tn), lambda l, j: (l, 0, j)),
        compiler_params=_params(),
        name="ada_mod",
    )(c_all, ada_w, ada_b.reshape(DEPTH, 1, 6 * D_MODEL))


def _mod_spec(mod, tm, rows_per_group):
    if mod.ndim == 3:
        tiles = rows_per_group // tm
        return pl.BlockSpec((1, 1, D_MODEL), lambda i: (i // tiles, 0, 0))
    return pl.BlockSpec((tm, D_MODEL), lambda i: (i, 0))


def _mod_val(ref):
    return ref[0] if len(ref.shape) == 3 else ref[...]


def _rms_mod(x, g, sc, sh):
    ms = jnp.mean(x * x, axis=-1, keepdims=True)
    return (x * lax.rsqrt(ms + EPS) * g) * (1.0 + sc) + sh


def _inproj_kernel(x_ref, g_ref, sc_ref, sh_ref, w_ref, u_ref, u16_ref):
    h = _rms_mod(x_ref[...], g_ref[...], _mod_val(sc_ref), _mod_val(sh_ref))
    r = _mm(h.astype(BF), w_ref[...])
    u_ref[...] = r
    u16_ref[...] = r[:, 6 * GW:10 * GW].astype(BF)


def _inproj(x, g, sc, sh, w16, rows_per_group, tm):
    n = x.shape[0]
    return pl.pallas_call(
        _inproj_kernel,
        out_shape=(jax.ShapeDtypeStruct((n, IN_COLS), F32),
                   jax.ShapeDtypeStruct((n, 4 * GW), BF)),
        grid=(n // tm,),
        in_specs=[
            pl.BlockSpec((tm, D_MODEL), lambda i: (i, 0)),
            pl.BlockSpec((1, D_MODEL), lambda i: (0, 0)),
            _mod_spec(sc, tm, rows_per_group),
            _mod_spec(sh, tm, rows_per_group),
            pl.BlockSpec((D_MODEL, IN_COLS), lambda i: (0, 0)),
        ],
        out_specs=(pl.BlockSpec((tm, IN_COLS), lambda i: (i, 0)),
                   pl.BlockSpec((tm, 4 * GW), lambda i: (i, 0))),
        compiler_params=_params(),
        name="in_proj",
    )(x, g, sc, sh, w16)


def _hgrn_consts(c):
    r = np.arange(c)
    mats = [(r[None, :] <= r[:, None]), (r[None, :] > r[:, None])]
    masks = []
    w = c // 2
    while w >= 1:
        blk = r // (2 * w)
        half = (r // w) % 2
        mid = blk * 2 * w + w - 1
        t_rows = (r[None, :] > mid[:, None]) & (r[None, :] <= r[:, None])
        s_rows = (r[None, :] > r[:, None]) & (r[None, :] <= mid[:, None])
        mats.append(np.where(half[:, None] == 1, t_rows, s_rows))
        masks.append((blk[:, None] == blk[None, :]) & (half[:, None] == 1) & (half[None, :] == 0))
        w //= 2
    masks.append(r[:, None] == r[None, :])
    wmat = np.concatenate(mats, axis=0).astype(np.float32)
    return jnp.asarray(wmat, BF), jnp.asarray(np.stack(masks).astype(np.float32))


def _hgrn_kernel(q_ref, f_ref, i_ref, g_ref, lb_ref, hg_ref, w_ref, mk_ref, bd_ref, s0_ref,
                 o_ref, st_ref, *, groups, chunk, cps, nlev):
    c = chunk

    @pl.when(pl.program_id(1) == 0)
    def _():
        st_ref[...] = s0_ref[...]

    lbb = jnp.clip(lb_ref[...], 0.0, 1.0)
    wmat = w_ref[...]
    for g in range(groups):
        a = f_ref[g]
        q_all = q_ref[g]
        iv16_all = i_ref[g].astype(BF)
        f = lbb + (1.0 - lbb) * _sigmoid(a)
        lf = jnp.log(jnp.maximum(f, F_FLOOR))
        kk_all = (1.0 - lbb) * _sigmoid(-a)
        lf_cat = jnp.concatenate([lf[cc * c:(cc + 1) * c] for cc in range(cps)], axis=1)
        e_cat = _mm_exact_lhs(wmat, lf_cat)
        st = st_ref[g]
        outs = []
        for cc in range(cps):
            e = e_cat[:, cc * GW:(cc + 1) * GW]
            q = q_all[cc * c:(cc + 1) * c]
            kk = kk_all[cc * c:(cc + 1) * c]
            iv16 = iv16_all[cc * c:(cc + 1) * c]
            b = e[0:c]
            qb16 = (q * jnp.exp(b)).astype(BF)
            kr16 = (kk * jnp.exp(e[c:2 * c])).astype(BF)
            dec_last = jnp.exp(b[c - 1:c])
            qt = [(q * jnp.exp(e[(2 + l) * c:(3 + l) * c])).astype(BF) for l in range(nlev)]
            kt = [(kk * jnp.exp(e[(2 + l) * c:(3 + l) * c])).astype(BF) for l in range(nlev)]
            qt.append(q.astype(BF))
            kt.append(kk.astype(BF))
            st16 = st.astype(BF)
            o_heads, st_heads = [], []
            for h in range(NH):
                sl = slice(h * HEAD_DIM, (h + 1) * HEAD_DIM)
                att = jnp.zeros((c, c), F32)
                for l in range(nlev + 1):
                    att = att + mk_ref[l] * _mm_nt(qt[l][:, sl], kt[l][:, sl])
                o_heads.append(_mm(att.astype(BF), iv16[:, sl]) + _mm_nt(qb16[:, sl], st16[:, sl]))
                st_heads.append(st[:, sl] * dec_last[:, sl] + _mm_tn(iv16[:, sl], kr16[:, sl]))
            outs.append(jnp.concatenate(o_heads, axis=1))
            st = jnp.concatenate(st_heads, axis=1)
        st_ref[g] = st
        o = jnp.concatenate(outs, axis=0)
        oh, ol = _split2(o * o)
        ms = _mm(oh, bd_ref[...]) + _mm(ol, bd_ref[...])
        o_ref[g] = o * lax.rsqrt(ms + EPS) * hg_ref[...] * _silu(g_ref[g])


def _hgrn(u3, lb, hg, s0t, chunk, groups, cps):
    bsz, t, _ = u3.shape
    nlev = int(math.log2(chunk))
    wmat, masks = _hgrn_consts(chunk)
    bd = jnp.asarray(np.kron(np.eye(NH), np.full((HEAD_DIM, HEAD_DIM), 1.0 / HEAD_DIM)), BF)
    rows = chunk * cps
    col = lambda k: pl.BlockSpec((groups, rows, GW), lambda b, c: (b, c, k))
    const = lambda shape: pl.BlockSpec(shape, lambda b, c: (0,) * len(shape))
    return pl.pallas_call(
        functools.partial(_hgrn_kernel, groups=groups, chunk=chunk, cps=cps, nlev=nlev),
        out_shape=(jax.ShapeDtypeStruct((bsz, t, GW), F32),
                   jax.ShapeDtypeStruct((bsz, HEAD_DIM, GW), F32)),
        grid=(bsz // groups, t // rows),
        in_specs=[col(0), col(1), col(2), col(3), const((1, GW)), const((1, GW)),
                  const(wmat.shape), const(masks.shape), const((GW, GW)),
                  pl.BlockSpec((groups, HEAD_DIM, GW), lambda b, c: (b, 0, 0))],
        out_specs=(pl.BlockSpec((groups, rows, GW), lambda b, c: (b, c, 0)),
                   pl.BlockSpec((groups, HEAD_DIM, GW), lambda b, c: (b, 0, 0))),
        compiler_params=_params(),
        name="hgrn2",
    )(u3, u3, u3, u3, lb, hg, wmat, masks, bd, s0t)


def _rope_tables(pos):
    half = ROT_DIM // 2
    inv = np.exp(-math.log(ROPE_THETA) * np.arange(half, dtype=np.float32) * 2.0 / ROT_DIM)
    ang = jnp.asarray(pos, F32)[:, None] * jnp.asarray(inv)[None, :]
    cos, sin = jnp.cos(ang), jnp.sin(ang)
    n = ang.shape[0]
    ones = jnp.ones((n, HEAD_DIM - ROT_DIM), F32)
    zeros = jnp.zeros((n, HEAD_DIM - ROT_DIM), F32)
    ch = jnp.concatenate([cos, cos, ones], axis=1)
    sh = jnp.concatenate([-sin, sin, zeros], axis=1)
    return jnp.tile(ch, (1, NH)), jnp.tile(sh, (1, NH))


def _rope_kernel(q_ref, k_ref, cos_ref, sin_ref, qo_ref, ko_ref, k16_ref, km_ref):
    cos = cos_ref[...]
    sin = sin_ref[...]
    lane = lax.broadcasted_iota(jnp.int32, cos.shape, 1) % HEAD_DIM
    first = lane < ROT_DIM // 2

    def rot(x):
        partner = jnp.where(first, pltpu.roll(x, GW - ROT_DIM // 2, 1), pltpu.roll(x, ROT_DIM // 2, 1))
        return x * cos + partner * sin

    qo_ref[...] = rot(q_ref[...])
    k = rot(k_ref[...])
    ko_ref[...] = k
    k16_ref[...] = k.astype(BF)
    km_ref[0] = jnp.mean(k, axis=0, keepdims=True)


def _rope(u, cos, sin, nbatch, rows):
    n = u.shape[0]
    t = n // nbatch
    nt = t // rows
    return pl.pallas_call(
        _rope_kernel,
        out_shape=(jax.ShapeDtypeStruct((n, GW), F32), jax.ShapeDtypeStruct((n, GW), F32),
                   jax.ShapeDtypeStruct((n, GW), BF), jax.ShapeDtypeStruct((n // rows, 1, GW), F32)),
        grid=(nbatch, nt),
        in_specs=[pl.BlockSpec((rows, GW), lambda b, j: (b * nt + j, 4)),
                  pl.BlockSpec((rows, GW), lambda b, j: (b * nt + j, 5)),
                  pl.BlockSpec((rows, GW), lambda b, j: (j, 0)),
                  pl.BlockSpec((rows, GW), lambda b, j: (j, 0))],
        out_specs=(pl.BlockSpec((rows, GW), lambda b, j: (b * nt + j, 0)),
                   pl.BlockSpec((rows, GW), lambda b, j: (b * nt + j, 0)),
                   pl.BlockSpec((rows, GW), lambda b, j: (b * nt + j, 0)),
                   pl.BlockSpec((1, 1, GW), lambda b, j: (b * nt + j, 0, 0))),
        compiler_params=_params(),
        name="rope",
    )(u, u, cos, sin)


ATTN_KEY_STEP = 256


def _topk_rows(scores, valid, idx):
    cur = jnp.where(valid, scores, NEG_BIG)
    picked = jnp.zeros(scores.shape, F32)
    big = jnp.int32(1 << 30)
    for _ in range(MOBA_TOPK):
        mx = jnp.max(cur, axis=0, keepdims=True)
        first = jnp.min(jnp.where(cur == mx, idx, big), axis=0, keepdims=True)
        pick = idx == first
        picked = jnp.where(pick & valid, 1.0, picked)
        cur = jnp.where(pick, -jnp.inf, cur)
    return picked


def _pair_weights(qt16, tq):
    z = jnp.zeros((HEAD_DIM, tq), qt16.dtype)
    out = []
    for p in range(NH // 2):
        a = qt16[(2 * p) * HEAD_DIM:(2 * p + 1) * HEAD_DIM]
        b = qt16[(2 * p + 1) * HEAD_DIM:(2 * p + 2) * HEAD_DIM]
        out.append(jnp.concatenate([jnp.concatenate([a, z], axis=1), jnp.concatenate([z, b], axis=1)], axis=0))
    return out


def _moba_kernel(q_ref, k_ref, vt_ref, km_ref, o_ref, m_ref, l_ref, acc_ref, sel_ref, *, nb):
    own = pl.program_id(1)
    q = q_ref[...]
    tq = q.shape[0]
    qt = jnp.transpose(q)
    rowi = lax.broadcasted_iota(jnp.int32, (nb, tq), 0)
    km = km_ref[...]
    for h in range(NH):
        sl = slice(h * HEAD_DIM, (h + 1) * HEAD_DIM)
        sel_ref[h] = _topk_rows(_mm3(km[:, sl], qt[sl, :]), rowi < own, rowi)
    qp = _pair_weights((qt * SCALE).astype(BF), tq)
    m_ref[...] = jnp.full(m_ref.shape, -jnp.inf, F32)
    l_ref[...] = jnp.zeros(l_ref.shape, F32)
    acc_ref[...] = jnp.zeros(acc_ref.shape, F32)

    def block(start, keep_fn):
        for r0 in range(0, MOBA_BLOCK, ATTN_KEY_STEP):
            at = pl.multiple_of(start + r0, ATTN_KEY_STEP)
            kblk = k_ref[pl.ds(at, ATTN_KEY_STEP), :]
            for p in range(NH // 2):
                sp = _mm(kblk[:, p * 2 * HEAD_DIM:(p + 1) * 2 * HEAD_DIM], qp[p])
                for r in range(2):
                    h = 2 * p + r
                    rows = slice(h * HEAD_DIM, (h + 1) * HEAD_DIM)
                    s = jnp.where(keep_fn(h, r0), sp[:, r * tq:(r + 1) * tq], NEG_BIG)
                    m_old = m_ref[h, 0:1, :]
                    m_new = jnp.maximum(m_old, jnp.max(s, axis=0, keepdims=True))
                    pt = jnp.exp(s - m_new)
                    alpha = jnp.exp(m_old - m_new)
                    l_ref[h, 0:1, :] = alpha * l_ref[h, 0:1, :] + jnp.sum(pt, axis=0, keepdims=True)
                    vt = vt_ref[rows, pl.ds(at, ATTN_KEY_STEP)]
                    acc_ref[rows, :] = alpha * acc_ref[rows, :] + _mm(vt, pt.astype(BF))
                    m_ref[h, 0:1, :] = m_new

    def past(n, carry):
        block(pl.multiple_of(n * MOBA_BLOCK, MOBA_BLOCK), lambda h, r0: sel_ref[h, pl.ds(n, 1), :] > 0.5)
        return carry

    lax.fori_loop(0, own, past, 0)

    keyi = lax.broadcasted_iota(jnp.int32, (ATTN_KEY_STEP, tq), 0)
    qi = lax.broadcasted_iota(jnp.int32, (ATTN_KEY_STEP, tq), 1)
    block(pl.multiple_of(own * MOBA_BLOCK, MOBA_BLOCK), lambda h, r0: keyi + r0 <= qi)
    out_t = jnp.concatenate(
        [acc_ref[h * HEAD_DIM:(h + 1) * HEAD_DIM, :] / l_ref[h, 0:1, :] for h in range(NH)], axis=0)
    o_ref[...] = jnp.transpose(out_t)


def _moba_prompt(qrot, k16, vt16, kmean, nbatch):
    n = qrot.shape[0]
    t = n // nbatch
    nq = t // MOBA_BLOCK
    nb = t // MOBA_BLOCK
    return pl.pallas_call(
        functools.partial(_moba_kernel, nb=nb),
        out_shape=jax.ShapeDtypeStruct((n, GW), F32),
        grid=(nbatch, nq),
        in_specs=[pl.BlockSpec((MOBA_BLOCK, GW), lambda b, j: (b * nq + j, 0)),
                  pl.BlockSpec((t, GW), lambda b, j: (b, 0)),
                  pl.BlockSpec((GW, t), lambda b, j: (b, 0)),
                  pl.BlockSpec((nb, GW), lambda b, j: (b, 0))],
        out_specs=pl.BlockSpec((MOBA_BLOCK, GW), lambda b, j: (b * nq + j, 0)),
        scratch_shapes=[pltpu.VMEM((NH, 8, MOBA_BLOCK), F32), pltpu.VMEM((NH, 8, MOBA_BLOCK), F32),
                        pltpu.VMEM((GW, MOBA_BLOCK), F32), pltpu.VMEM((NH, nb, MOBA_BLOCK), F32)],
        compiler_params=_params(),
        name="moba_prompt",
    )(qrot, k16, vt16, kmean)


SB_BLOCK = 256
SB_CUTOFF = 104.0


def _later_mask(n, rows_are_keys):
    r = np.arange(n)
    m = r[None, :] > r[:, None]
    return jnp.asarray((m if rows_are_keys else m.T).astype(np.float32), BF)


def _sb_kernel(q_ref, k_ref, vt_ref, up_ref, o_ref, carry_ref, acc_ref):
    j = pl.program_id(1)
    tq = q_ref.shape[0]
    qp = _pair_weights((jnp.transpose(q_ref[...]) * SCALE).astype(BF), tq)
    lower = up_ref[...]
    carry_ref[...] = jnp.zeros(carry_ref.shape, F32)
    acc_ref[...] = jnp.zeros(acc_ref.shape, F32)

    def block(start, mask):
        kblk = k_ref[pl.ds(start, SB_BLOCK), :]
        for p in range(NH // 2):
            zp = _mm(kblk[:, p * 2 * HEAD_DIM:(p + 1) * 2 * HEAD_DIM], qp[p])
            for r in range(2):
                h = 2 * p + r
                rows = slice(h * HEAD_DIM, (h + 1) * HEAD_DIM)
                lpos, lneg = _log_sigmoid_pair(zp[:, r * tq:(r + 1) * tq])
                if mask is not None:
                    lneg = jnp.where(mask, lneg, 0.0)
                hi, lo = _split2(lneg)
                rest = carry_ref[h:h + 1, :] + _mm(lower, hi) + _mm(lower, lo)
                a = jnp.exp(lpos + jnp.minimum(rest, 0.0))
                if mask is not None:
                    a = jnp.where(mask, a, 0.0)
                vt = vt_ref[rows, pl.ds(start, SB_BLOCK)]
                acc_ref[rows, :] = acc_ref[rows, :] + _mm(vt, a.astype(BF))
                carry_ref[h:h + 1, :] = carry_ref[h:h + 1, :] + jnp.sum(lneg, axis=0, keepdims=True)

    def active():
        return (jnp.max(carry_ref[0:NH, :]) > -SB_CUTOFF).astype(jnp.int32)

    keyi = lax.broadcasted_iota(jnp.int32, (SB_BLOCK, tq), 0)
    qi = lax.broadcasted_iota(jnp.int32, (SB_BLOCK, tq), 1)
    block(pl.multiple_of(j * SB_BLOCK, SB_BLOCK), keyi < qi)

    def body(c):
        block(pl.multiple_of(c[0] * SB_BLOCK, SB_BLOCK), None)
        return c[0] - 1, active()

    lax.while_loop(lambda c: (c[0] >= 0) & (c[1] > 0), body, (j - 1, active()))
    o_ref[...] = jnp.transpose(acc_ref[...])


def _sb_prompt(u, u16, vt16, nbatch):
    n = u.shape[0]
    t = n // nbatch
    nq = t // SB_BLOCK
    return pl.pallas_call(
        _sb_kernel,
        out_shape=jax.ShapeDtypeStruct((n, GW), F32),
        grid=(nbatch, nq),
        in_specs=[pl.BlockSpec((SB_BLOCK, GW), lambda b, j: (b * nq + j, 7)),
                  pl.BlockSpec((t, GW), lambda b, j: (b, 2)),
                  pl.BlockSpec((GW, t), lambda b, j: (b, 0)),
                  pl.BlockSpec((SB_BLOCK, SB_BLOCK), lambda b, j: (0, 0))],
        out_specs=pl.BlockSpec((SB_BLOCK, GW), lambda b, j: (b * nq + j, 0)),
        scratch_shapes=[pltpu.VMEM((8, SB_BLOCK), F32), pltpu.VMEM((GW, SB_BLOCK), F32)],
        compiler_params=_params(),
        name="sb_prompt",
    )(u, u16, vt16, _later_mask(SB_BLOCK, True))


def _block_diag_q(q, nbatch, t):
    q4 = q.reshape(nbatch, t, NH, HEAD_DIM)
    eye = jnp.eye(NH, dtype=q.dtype)
    bd = q4[:, None, :, :, :] * eye[None, :, None, :, None]
    return bd.reshape(nbatch, NH * t, GW)


def _page_specs(layer, n_phys, n_pages):
    def spec(p):
        return pl.BlockSpec((1, GW, PAGE_SIZE), lambda b, pt: (layer * n_phys + pt[b * n_pages + p], 0, 0))
    return [spec(p) for p in range(n_pages)]


def _heads_out(acc, den, t):
    return jnp.concatenate(
        [acc[h * t:(h + 1) * t, h * HEAD_DIM:(h + 1) * HEAD_DIM] / den[h * t:(h + 1) * t]
         for h in range(NH)], axis=1)


def _pad_rows(x, rows):
    return jnp.concatenate([x, jnp.zeros((rows - x.shape[0], x.shape[1]), x.dtype)], axis=0)


def _moba_dec_kernel(pt_ref, qbd_ref, knew_ref, vnew_ref, *refs, n_pages, t, own):
    del pt_ref
    kt_refs, vt_refs, o_ref = refs[:n_pages], refs[n_pages:2 * n_pages], refs[2 * n_pages]
    qbd = qbd_ref[0]
    nrow = qbd.shape[0]
    ppb = MOBA_BLOCK // PAGE_SIZE
    lg = [_mm3(qbd, kt_refs[p][0]) for p in range(n_pages)]
    score = []
    for n in range(own):
        tot = jnp.sum(lg[n * ppb], axis=1, keepdims=True)
        for r in range(1, ppb):
            tot = tot + jnp.sum(lg[n * ppb + r], axis=1, keepdims=True)
        score.append(tot * (1.0 / MOBA_BLOCK))
    sel = []
    for n in range(own):
        rank = jnp.zeros((nrow, 1), F32)
        for m in range(own):
            if m != n:
                beats = (score[m] >= score[n]) if m < n else (score[m] > score[n])
                rank = rank + jnp.where(beats, 1.0, 0.0)
        sel.append(rank < float(MOBA_TOPK))
    logits = [jnp.where(sel[p // ppb], lg[p] * SCALE, NEG_BIG) for p in range(n_pages)]
    lnew = _mm3_nt(qbd, _pad_rows(knew_ref[...], PAGE_SIZE)) * SCALE
    keyi = lax.broadcasted_iota(jnp.int32, lnew.shape, 1)
    qi = lax.broadcasted_iota(jnp.int32, lnew.shape, 0) % t
    lnew = jnp.where(keyi <= qi, lnew, NEG_BIG)
    m = jnp.max(lnew, axis=1, keepdims=True)
    for lp in logits:
        m = jnp.maximum(m, jnp.max(lp, axis=1, keepdims=True))
    enew = jnp.exp(lnew - m)
    den = jnp.sum(enew, axis=1, keepdims=True)
    acc = _mm(enew.astype(BF), _pad_rows(vnew_ref[...], PAGE_SIZE).astype(BF))
    for p in range(n_pages):
        e = jnp.exp(logits[p] - m)
        den = den + jnp.sum(e, axis=1, keepdims=True)
        acc = acc + _mm_nt(e.astype(BF), vt_refs[p][0].astype(BF))
    o_ref[...] = _heads_out(acc, den, t)


def _sb_dec_kernel(pt_ref, qbd_ref, knew_ref, vnew_ref, later_ref, *refs, n_pages, t):
    del pt_ref
    kt_refs, vt_refs, o_ref = refs[:n_pages], refs[n_pages:2 * n_pages], refs[2 * n_pages]
    qbd16 = (qbd_ref[0] * SCALE).astype(BF)
    nrow = qbd16.shape[0]
    later = later_ref[...]

    def block(z, carry, mask):
        lpos, lneg = _log_sigmoid_pair(z)
        if mask is not None:
            lneg = jnp.where(mask, lneg, 0.0)
        hi, lo = _split2(lneg)
        rest = carry + _mm(hi, later) + _mm(lo, later)
        a = jnp.exp(lpos + jnp.minimum(rest, 0.0))
        if mask is not None:
            a = jnp.where(mask, a, 0.0)
        return a.astype(BF), carry + jnp.sum(lneg, axis=1, keepdims=True)

    keyi = lax.broadcasted_iota(jnp.int32, (nrow, PAGE_SIZE), 1)
    qi = lax.broadcasted_iota(jnp.int32, (nrow, PAGE_SIZE), 0) % t
    znew = _mm_nt(qbd16, _pad_rows(knew_ref[...], PAGE_SIZE).astype(BF))
    a, carry = block(znew, jnp.zeros((nrow, 1), F32), keyi < qi)
    acc = _mm(a, _pad_rows(vnew_ref[...], PAGE_SIZE).astype(BF))
    for p in range(n_pages - 1, -1, -1):
        def visit(state, p=p):
            a, carry_new = block(_mm(qbd16, kt_refs[p][0].astype(BF)), state[1], None)
            return state[0] + _mm_nt(a, vt_refs[p][0].astype(BF)), carry_new
        acc, carry = lax.cond(jnp.max(carry) > -SB_CUTOFF, visit, lambda state: state, (acc, carry))
    o_ref[...] = _heads_out(acc, jnp.ones((nrow, 1), F32), t)


def _dec_attention(kind, layer, page_table, qbd, knew, vnew, kcache, vcache, t, pos0):
    nbatch, n_pages = page_table.shape
    n_phys = kcache.shape[0] // DEPTH
    assert pos0 == n_pages * PAGE_SIZE and pos0 % MOBA_BLOCK == 0 and t <= PAGE_SIZE
    pspecs = _page_specs(layer, n_phys, n_pages)
    row = pl.BlockSpec((t, GW), lambda b, pt: (b, 0))
    in_specs = [pl.BlockSpec((1, NH * t, GW), lambda b, pt: (b, 0, 0)), row, row]
    args = [qbd, knew, vnew]
    if kind == "moba":
        body = functools.partial(_moba_dec_kernel, n_pages=n_pages, t=t, own=pos0 // MOBA_BLOCK)
    else:
        body = functools.partial(_sb_dec_kernel, n_pages=n_pages, t=t)
        in_specs.append(pl.BlockSpec((PAGE_SIZE, PAGE_SIZE), lambda b, pt: (0, 0)))
        args.append(_later_mask(PAGE_SIZE, False))
    in_specs += pspecs + pspecs
    args += [kcache] * n_pages + [vcache] * n_pages
    return pl.pallas_call(
        body,
        out_shape=jax.ShapeDtypeStruct((nbatch * t, GW), F32),
        grid_spec=pltpu.PrefetchScalarGridSpec(
            num_scalar_prefetch=1, grid=(nbatch,), in_specs=in_specs,
            out_specs=pl.BlockSpec((t, GW), lambda b, pt: (b, 0))),
        compiler_params=_params(),
        name=kind + "_decode",
    )(page_table.reshape(-1), *args)


POOL_HALO = 16


def _pool_bands(seqs, t):
    halo_seq = np.repeat(np.arange(seqs), POOL_HALO)
    halo_e = np.tile(np.arange(POOL_HALO), seqs)
    row_seq = np.repeat(np.arange(seqs), t)
    row_e = np.tile(np.arange(t), seqs) + POOL_HALO
    col_seq = np.concatenate([halo_seq, row_seq])[None, :]
    col_e = np.concatenate([halo_e, row_e])[None, :]
    same = col_seq == row_seq[:, None]
    r = row_e[:, None]
    return jnp.asarray(np.stack([(same & (col_e <= r) & (col_e > r - w)).astype(np.float32)
                                 for w in POOL_WINDOWS]), BF)


def _pool_kernel(u_ref, halo_ref, buf_ref, band_ref, pw_ref, ps_ref, o_ref, *, rows, nt, t_seq, pos0):
    i = pl.program_id(0)
    u = u_ref[...]
    halo = jnp.where(i % nt == 0, buf_ref[...], halo_ref[...])
    ext = jnp.concatenate([halo, u], axis=0)
    hi, mid, lo = _split3(ext)
    tok = (i * rows + lax.broadcasted_iota(jnp.int32, (rows, 1), 0)) % t_seq
    pos = (pos0 + tok).astype(F32)
    parts = []
    gc = GW // len(POOL_WINDOWS)
    for gi, w in enumerate(POOL_WINDOWS):
        sl = slice(gi * gc, (gi + 1) * gc)
        band = band_ref[gi]
        ssum = _mm(band, hi[:, sl]) + _mm(band, mid[:, sl]) + _mm(band, lo[:, sl])
        cnt = jnp.minimum(float(w), pos + 1.0)
        parts.append(ssum / cnt - u[:, sl])
    d = jnp.concatenate(parts, axis=1)
    o_ref[...] = _mm(d.astype(BF), pw_ref[...]) * ps_ref[...]


def _pool(u, buf_flat, pw_bd16, ps, nbatch, t, pos0):
    n = u.shape[0]
    if t >= MOBA_BLOCK:
        rows, seqs, hr = 256, 1, POOL_HALO
        nt = t // rows
        band = _pool_bands(1, rows)
        halo_arr = u
        halo_spec = pl.BlockSpec((hr, GW), lambda i: (jnp.maximum(i * (rows // hr) - 1, 0), 10))
        buf_spec = pl.BlockSpec((hr, GW), lambda i: (i // nt, 0))
    else:
        seqs = 128 // t
        rows, hr, nt = seqs * t, seqs * POOL_HALO, 1
        band = _pool_bands(seqs, t)
        halo_arr = buf_flat
        halo_spec = pl.BlockSpec((hr, GW), lambda i: (i, 0))
        buf_spec = halo_spec
    return pl.pallas_call(
        functools.partial(_pool_kernel, rows=rows, nt=nt, t_seq=t, pos0=pos0),
        out_shape=jax.ShapeDtypeStruct((n, GW), F32),
        grid=(n // rows,),
        in_specs=[pl.BlockSpec((rows, GW), lambda i: (i, 10)), halo_spec, buf_spec,
                  pl.BlockSpec(band.shape, lambda i: (0, 0, 0)),
                  pl.BlockSpec((GW, GW), lambda i: (0, 0)),
                  pl.BlockSpec((1, GW), lambda i: (0, 0))],
        out_specs=pl.BlockSpec((rows, GW), lambda i: (i, 0)),
        compiler_params=_params(),
        name="pool_mixer",
    )(u, halo_arr, buf_flat, band, pw_bd16, ps)


def _route(logits):
    lane = lax.broadcasted_iota(jnp.int32, logits.shape, 1)
    big = jnp.int32(1 << 30)
    gmask = lane < N_GROUPS
    gl = jnp.where(gmask, logits, -jnp.inf)
    gmax = jnp.max(gl, axis=-1, keepdims=True)
    gsum = jnp.sum(jnp.where(gmask, jnp.exp(gl - gmax), 0.0), axis=-1, keepdims=True)
    g_top = 1.0 / gsum
    g_idx = jnp.min(jnp.where(gl == gmax, lane, big), axis=-1, keepdims=True)
    emask = (lane >= N_GROUPS) & (lane < N_GROUPS + N_EXPERTS) & ((lane - N_GROUPS) // EXP_PER_GROUP == g_idx)
    el = jnp.where(emask, logits, -jnp.inf)
    m1 = jnp.max(el, axis=-1, keepdims=True)
    i1 = jnp.min(jnp.where(el == m1, lane, big), axis=-1, keepdims=True)
    el2 = jnp.where(lane == i1, -jnp.inf, el)
    m2 = jnp.max(el2, axis=-1, keepdims=True)
    i2 = jnp.min(jnp.where(el2 == m2, lane, big), axis=-1, keepdims=True)
    r = jnp.exp(m2 - m1)
    w1 = g_top / (1.0 + r)
    w2 = g_top * r / (1.0 + r)
    gate = jnp.where(lane == i1, w1, jnp.where(lane == i2, w2, 0.0))
    info = jnp.where(lane == 0, (i1 - N_GROUPS).astype(F32),
                     jnp.where(lane == 1, (i2 - N_GROUPS).astype(F32),
                               jnp.where(lane == 2, w1, jnp.where(lane == 3, w2, 0.0))))
    return gate, info


def _outproj_kernel(oa_ref, ob_ref, oc_ref, od_ref, x_ref, gt_ref, g_ref, sc_ref, sh_ref, w_ref,
                    wr_ref, br_ref, x1_ref, h_ref, gate_ref, info_ref):
    merged = jnp.concatenate([oa_ref[...], ob_ref[...], oc_ref[...], od_ref[...]], axis=1).astype(BF)
    x1 = x_ref[...] + _mod_val(gt_ref) * _mm(merged, w_ref[...])
    x1_ref[...] = x1
    h = _rms_mod(x1, g_ref[...], _mod_val(sc_ref), _mod_val(sh_ref))
    h_ref[...] = h.astype(h_ref.dtype)
    gate_ref[...], info_ref[...] = _route(_mm3(h, wr_ref[...]) + br_ref[...])


def _outproj(oa, ob, oc, od, x, gt, g, sc, sh, w16, wr, br, rows_per_group, tm, h_dtype):
    n = x.shape[0]
    quarter = pl.BlockSpec((tm, GW), lambda i: (i, 0))
    full = pl.BlockSpec((tm, D_MODEL), lambda i: (i, 0))
    narrow = pl.BlockSpec((tm, LANES), lambda i: (i, 0))
    return pl.pallas_call(
        _outproj_kernel,
        out_shape=(jax.ShapeDtypeStruct((n, D_MODEL), F32), jax.ShapeDtypeStruct((n, D_MODEL), h_dtype),
                   jax.ShapeDtypeStruct((n, LANES), F32), jax.ShapeDtypeStruct((n, LANES), F32)),
        grid=(n // tm,),
        in_specs=[quarter, quarter, quarter, quarter, full,
                  _mod_spec(gt, tm, rows_per_group),
                  pl.BlockSpec((1, D_MODEL), lambda i: (0, 0)),
                  _mod_spec(sc, tm, rows_per_group), _mod_spec(sh, tm, rows_per_group),
                  pl.BlockSpec((D_MODEL, D_MODEL), lambda i: (0, 0)),
                  pl.BlockSpec((D_MODEL, LANES), lambda i: (0, 0)),
                  pl.BlockSpec((1, LANES), lambda i: (0, 0))],
        out_specs=(full, full, narrow, narrow),
        compiler_params=_params(),
        name="out_proj_router",
    )(oa, ob, oc, od, x, gt, g, sc, sh, w16, wr, br)


def _moe_kernel(h_ref, gate_ref, x1_ref, gt_ref, wg_ref, wu_ref, wd_ref, o_ref, acc_ref):
    e = pl.program_id(1)

    @pl.when(e == 0)
    def _():
        acc_ref[...] = jnp.zeros(acc_ref.shape, F32)

    h = h_ref[...]
    a = _silu(_mm(h, wg_ref[0])) * _mm(h, wu_ref[0])
    lane = lax.broadcasted_iota(jnp.int32, gate_ref.shape, 1)
    gcol = jnp.sum(jnp.where(lane == e + N_GROUPS, gate_ref[...], 0.0), axis=-1, keepdims=True)
    acc_ref[...] += gcol * _mm(a.astype(BF), wd_ref[0])

    @pl.when(e == N_EXPERTS - 1)
    def _():
        o_ref[...] = x1_ref[...] + _mod_val(gt_ref) * acc_ref[...]


def _moe(h16, gate, x1, gt, wg16, wu16, wd16, layer, rows_per_group, tm):
    n = x1.shape[0]
    full = lambda: pl.BlockSpec((tm, D_MODEL), lambda i, e: (i, 0))
    if gt.ndim == 3:
        tiles = rows_per_group // tm
        gt_spec = pl.BlockSpec((1, 1, D_MODEL), lambda i, e: (i // tiles, 0, 0))
    else:
        gt_spec = pl.BlockSpec((tm, D_MODEL), lambda i, e: (i, 0))
    return pl.pallas_call(
        _moe_kernel,
        out_shape=jax.ShapeDtypeStruct((n, D_MODEL), F32),
        grid=(n // tm, N_EXPERTS),
        in_specs=[full(), pl.BlockSpec((tm, LANES), lambda i, e: (i, 0)), full(), gt_spec,
                  pl.BlockSpec((1, D_MODEL, D_EXPERT), lambda i, e: (layer * N_EXPERTS + e, 0, 0)),
                  pl.BlockSpec((1, D_MODEL, D_EXPERT), lambda i, e: (layer * N_EXPERTS + e, 0, 0)),
                  pl.BlockSpec((1, D_EXPERT, D_MODEL), lambda i, e: (layer * N_EXPERTS + e, 0, 0))],
        out_specs=full(),
        scratch_shapes=[pltpu.VMEM((tm, D_MODEL), F32)],
        compiler_params=_params(),
        name="moe_experts",
    )(h16, gate, x1, gt, wg16, wu16, wd16)


MOE_TILE = 256
MOE_COMBINE_TILE = 256
DMA_UNROLL = 8


def _moe_plan(info, tile):
    n = info.shape[0]
    e = info[:, 0:2].astype(jnp.int32).reshape(-1)
    onehot = (e[:, None] == jnp.arange(N_EXPERTS, dtype=jnp.int32)[None, :]).astype(jnp.int32)
    csum = jnp.cumsum(onehot, axis=0)
    counts = csum[-1]
    rank = jnp.sum((csum - onehot) * onehot, axis=1)
    padded = ((counts + tile - 1) // tile) * tile
    ends = jnp.cumsum(padded)
    pos = jnp.sum(onehot * (ends - padded)[None, :], axis=1) + rank
    total = 2 * n + N_EXPERTS * tile
    src = jnp.zeros((total,), jnp.int32).at[pos].set(jnp.arange(2 * n, dtype=jnp.int32) // 2)
    tstart = jnp.arange(total // tile, dtype=jnp.int32) * tile
    texp = jnp.minimum(jnp.sum((tstart[:, None] >= ends[None, :]).astype(jnp.int32), axis=1), N_EXPERTS - 1)
    used = (ends[-1] // tile).astype(jnp.int32).reshape(1)
    return src, texp.astype(jnp.int32), used, pos.astype(jnp.int32)


def _gather_rows(idx_ref, base, count, src_hbm, dst, sem, unroll, row_of):
    def body(it, carry):
        for k in range(unroll):
            aligned, off = row_of(it, k)
            row = pl.multiple_of(aligned, 8) + off
            pltpu.make_async_copy(src_hbm.at[pl.ds(idx_ref[base + it * unroll + k], 1)],
                                  dst.at[pl.ds(row, 1)], sem).start()
        return carry
    lax.fori_loop(0, count // unroll, body, 0)


def _moe_group_kernel(src_ref, te_ref, used_ref, h_hbm, wg_ref, wu_ref, wd_ref, y_ref, xbuf, sem, *, tile):
    del te_ref
    i = pl.program_id(0)
    slot = i % 2
    used = used_ref[0]

    def start(t, s):
        _gather_rows(src_ref, t * tile, tile, h_hbm, xbuf.at[s], sem.at[s], DMA_UNROLL,
                     lambda it, k: (it * DMA_UNROLL, k))

    @pl.when(i == 0)
    def _():
        start(0, 0)

    @pl.when(i + 1 < used)
    def _():
        start(i + 1, 1 - slot)

    @pl.when(i < used)
    def _():
        pltpu.make_async_copy(xbuf.at[slot], xbuf.at[slot], sem.at[slot]).wait()
        x = xbuf[slot].astype(BF)
        a = _silu(_mm(x, wg_ref[0])) * _mm(x, wu_ref[0])
        y_ref[...] = _mm(a.astype(BF), wd_ref[0])

    @pl.when(i >= used)
    def _():
        y_ref[...] = jnp.zeros(y_ref.shape, F32)


def _moe_combine_kernel(pos_ref, y_hbm, x1_ref, gt_ref, info_ref, o_ref, ybuf, sem, *, tc, nsteps):
    i = pl.program_id(0)
    slot = i % 2

    def start(t, s):
        _gather_rows(pos_ref, t * 2 * tc, 2 * tc, y_hbm, ybuf.at[s], sem.at[s], 2 * DMA_UNROLL,
                     lambda it, k: (it * DMA_UNROLL, (k % 2) * tc + k // 2))

    @pl.when(i == 0)
    def _():
        start(0, 0)

    @pl.when(i + 1 < nsteps)
    def _():
        start(i + 1, 1 - slot)

    pltpu.make_async_copy(ybuf.at[slot], ybuf.at[slot], sem.at[slot]).wait()
    info = info_ref[...]
    y = info[:, 2:3] * ybuf[slot, 0:tc, :] + info[:, 3:4] * ybuf[slot, tc:2 * tc, :]
    o_ref[...] = x1_ref[...] + _mod_val(gt_ref) * y


def _moe_sparse(h, info, x1, gt, wg16, wu16, wd16, layer, rows_per_group):
    n = x1.shape[0]
    tile, tc = MOE_TILE, MOE_COMBINE_TILE
    src, texp, used, pos = _moe_plan(info, tile)
    total = src.shape[0]
    wspec = lambda shape: pl.BlockSpec((1,) + shape, lambda i, s, te, u: (layer * N_EXPERTS + te[i], 0, 0))
    ys = pl.pallas_call(
        functools.partial(_moe_group_kernel, tile=tile),
        out_shape=jax.ShapeDtypeStruct((total, D_MODEL), F32),
        grid_spec=pltpu.PrefetchScalarGridSpec(
            num_scalar_prefetch=3, grid=(total // tile,),
            in_specs=[pl.BlockSpec(memory_space=pl.ANY), wspec((D_MODEL, D_EXPERT)), wspec((D_MODEL, D_EXPERT)),
                      wspec((D_EXPERT, D_MODEL))],
            out_specs=pl.BlockSpec((tile, D_MODEL), lambda i, s, te, u: (i, 0)),
            scratch_shapes=[pltpu.VMEM((2, tile, D_MODEL), F32), pltpu.SemaphoreType.DMA((2,))]),
        compiler_params=_params(),
        name="moe_grouped",
    )(src, texp, used, h, wg16, wu16, wd16)
    nsteps = n // tc
    tiles = rows_per_group // tc
    return pl.pallas_call(
        functools.partial(_moe_combine_kernel, tc=tc, nsteps=nsteps),
        out_shape=jax.ShapeDtypeStruct((n, D_MODEL), F32),
        grid_spec=pltpu.PrefetchScalarGridSpec(
            num_scalar_prefetch=1, grid=(nsteps,),
            in_specs=[pl.BlockSpec(memory_space=pl.ANY),
                      pl.BlockSpec((tc, D_MODEL), lambda i, p: (i, 0)),
                      pl.BlockSpec((1, 1, D_MODEL), lambda i, p: (i // tiles, 0, 0)),
                      pl.BlockSpec((tc, LANES), lambda i, p: (i, 0))],
            out_specs=pl.BlockSpec((tc, D_MODEL), lambda i, p: (i, 0)),
            scratch_shapes=[pltpu.VMEM((2, 2 * tc, D_MODEL), F32), pltpu.SemaphoreType.DMA((2,))]),
        compiler_params=_params(),
        name="moe_combine",
    )(pos, ys, x1, gt, info)


def _final_kernel(x_ref, g_ref, o_ref):
    x = x_ref[...]
    ms = jnp.mean(x * x, axis=-1, keepdims=True)
    o_ref[...] = x * lax.rsqrt(ms + EPS) * g_ref[...]


def _final_norm(x, g, tm):
    n = x.shape[0]
    return pl.pallas_call(
        _final_kernel,
        out_shape=jax.ShapeDtypeStruct((n, D_MODEL), F32),
        grid=(n // tm,),
        in_specs=[pl.BlockSpec((tm, D_MODEL), lambda i: (i, 0)), pl.BlockSpec((1, D_MODEL), lambda i: (0, 0))],
        out_specs=pl.BlockSpec((tm, D_MODEL), lambda i: (i, 0)),
        compiler_params=_params(),
        name="final_norm",
    )(x, g)


def _state_in(s):
    b = s.shape[0]
    return s.transpose(0, 3, 1, 2).reshape(b, HEAD_DIM, GW)


def _state_out(s):
    b = s.shape[0]
    return s.reshape(b, HEAD_DIM, NH, HEAD_DIM).transpose(0, 2, 3, 1)


def _hgrn_lower_bounds(lb_logits):
    p = jax.nn.softmax(lb_logits.astype(F32), axis=0)
    return jnp.cumsum(p, axis=0) - p[0:1]


def _layer(x, mods, wl, layer, nbatch, t, pos0, s0, pbuf, past, tm):
    sh1, sc1, gt1, sh2, sc2, gt2 = mods
    n = nbatch * t
    u, u16 = _inproj(x, wl["g_mix"], sc1, sh1, wl["w_in"], t, tm)
    u3 = u.reshape(nbatch, t, IN_COLS)

    chunk = HG_CHUNK if t % HG_CHUNK == 0 else t
    groups = 2 if t % HG_CHUNK == 0 else 8
    cps = HG_CHUNKS_PER_STEP if t % (HG_CHUNK * HG_CHUNKS_PER_STEP) == 0 else 1
    oa, st = _hgrn(u3, wl["lb"], wl["hg_g"], _state_in(s0), chunk, groups, cps)

    if past is None:
        cos, sin = _rope_tables(np.arange(t) + pos0)
        qrot, krot, krot16, kmean = _rope(u, cos, sin, nbatch, MOBA_BLOCK)
        u16t = u16.reshape(nbatch, t, 4 * GW).transpose(0, 2, 1)
        vtb = u16t[:, 0:GW].reshape(nbatch * GW, t)
        vtc = u16t[:, 3 * GW:4 * GW].reshape(nbatch * GW, t)
        ob = _moba_prompt(qrot, krot16, vtb, kmean.reshape(-1, GW), nbatch)
        oc = _sb_prompt(u, u16, vtc, nbatch)
    else:
        cos, sin = _rope_tables(np.tile(np.arange(t), nbatch) + pos0)
        qrot, krot, _, _ = _rope(u, cos, sin, 1, n)
        page_table, cmk, cmv, csk, csv = past
        vb = u[:, 6 * GW:7 * GW]
        ob = _dec_attention("moba", layer, page_table, _block_diag_q(qrot, nbatch, t), krot, vb, cmk, cmv, t, pos0)
        qc, kc, vc = u[:, 7 * GW:8 * GW], u[:, 8 * GW:9 * GW], u[:, 9 * GW:10 * GW]
        oc = _dec_attention("sb", layer, page_table, _block_diag_q(qc, nbatch, t), kc, vc, csk, csv, t, pos0)

    buf_flat = jnp.pad(pbuf, ((0, 0), (POOL_HALO - POOL_BUF, 0), (0, 0))).reshape(nbatch * POOL_HALO, GW)
    od = _pool(u, buf_flat, wl["pool_w"], wl["pool_scale"], nbatch, t, pos0)
    ud = u3[:, :, 10 * GW:]
    buf_new = jnp.concatenate([pbuf, ud], axis=1)[:, -POOL_BUF:]

    grouped = past is None and t % MOE_COMBINE_TILE == 0
    x1, h, gate, info = _outproj(oa.reshape(n, GW), ob, oc, od, x, gt1, wl["g_ffn"], sc2, sh2,
                                 wl["w_out"], wl["wr"], wl["br"], t, tm, F32 if grouped else BF)
    if grouped:
        x2 = _moe_sparse(h, info, x1, gt2, wl["ewg"], wl["ewu"], wl["ewd"], layer, t)
    else:
        x2 = _moe(h, gate, x1, gt2, wl["ewg"], wl["ewu"], wl["ewd"], layer, t, tm)

    rows = lambda z: z.reshape(nbatch, t, NH, HEAD_DIM)
    states = (rows(krot), rows(u[:, 6 * GW:7 * GW]), rows(u[:, 8 * GW:9 * GW]), rows(u[:, 9 * GW:10 * GW]),
              _state_out(st), buf_new)
    return x2, states


def kernel(x_prompt, x_sample, cache_moba_k, cache_moba_v, cache_sb_k, cache_sb_v, state_hgrn, state_pool,
           page_table, c_prompt, c_sample, ada_w, ada_b, norm_mix_g, norm_ffn_g, w_in, hgrn_lb, hgrn_norm_g,
           pool_w, pool_scale, w_out, router_group_w, router_group_b, router_expert_w, router_expert_b,
           expert_w_gate, expert_w_up, expert_w_down, final_norm_g):
    bp, tp, _ = x_prompt.shape
    bs, ts, _ = x_sample.shape
    pos0_s = page_table.shape[1] * PAGE_SIZE
    n_phys = cache_moba_k.shape[1]

    c_all = jnp.concatenate([c_prompt, c_sample], axis=0)
    mpad = (-c_all.shape[0]) % 8
    mod = _ada_all(jnp.pad(c_all, ((0, mpad), (0, 0))), ada_w, ada_b)

    lbs = _hgrn_lower_bounds(hgrn_lb)
    w_in16 = w_in.astype(BF)
    w_out16 = w_out.astype(BF)
    ewg16 = expert_w_gate.astype(BF).reshape(DEPTH * N_EXPERTS, D_MODEL, D_EXPERT)
    ewu16 = expert_w_up.astype(BF).reshape(DEPTH * N_EXPERTS, D_MODEL, D_EXPERT)
    ewd16 = expert_w_down.astype(BF).reshape(DEPTH * N_EXPERTS, D_EXPERT, D_MODEL)
    wr = jnp.concatenate([router_group_w, router_expert_w], axis=-1)
    wr = jnp.pad(wr, ((0, 0), (0, 0), (0, LANES - wr.shape[-1])))
    br = jnp.concatenate([router_group_b, router_expert_b], axis=-1)
    br = jnp.pad(br, ((0, 0), (0, LANES - br.shape[-1])))
    caches = tuple(c.transpose(0, 1, 3, 4, 2).reshape(DEPTH * n_phys, GW, PAGE_SIZE)
                   for c in (cache_moba_k, cache_moba_v, cache_sb_k, cache_sb_v))

    xp = x_prompt.reshape(bp * tp, D_MODEL)
    xs = x_sample.reshape(bs * ts, D_MODEL)
    st_p, st_s = [], []
    for l in range(DEPTH):
        pw_bd = jnp.zeros((GW, GW), F32)
        gc = GW // len(POOL_WINDOWS)
        for gi in range(len(POOL_WINDOWS)):
            pw_bd = lax.dynamic_update_slice(pw_bd, pool_w[l, gi], (gi * gc, gi * gc))
        wl = dict(g_mix=norm_mix_g[l][None], g_ffn=norm_ffn_g[l][None], w_in=w_in16[l], lb=lbs[l][None],
                  hg_g=hgrn_norm_g[l][None], pool_w=pw_bd.astype(BF), pool_scale=pool_scale[l][None],
                  w_out=w_out16[l], wr=wr[l], br=br[l][None], ewg=ewg16, ewu=ewu16, ewd=ewd16)
        mp = mod[l, :bp].reshape(bp, 6, 1, D_MODEL)
        mods_p = tuple(mp[:, k] for k in range(6))
        ms = jnp.repeat(mod[l, bp:bp + bs], ts, axis=0).reshape(bs * ts, 6, D_MODEL)
        mods_s = tuple(ms[:, k] for k in range(6))
        xp, sp = _layer(xp, mods_p, wl, l, bp, tp, 0, jnp.zeros((bp, NH, HEAD_DIM, HEAD_DIM), F32),
                        jnp.zeros((bp, POOL_BUF, GW), F32), None, min(ROW_TILE, tp))
        xs, ss = _layer(xs, mods_s, wl, l, bs, ts, pos0_s, state_hgrn[l], state_pool[l],
                        (page_table,) + caches, min(ROW_TILE, bs * ts))
        st_p.append(sp)
        st_s.append(ss)

    y_prompt = _final_norm(xp, final_norm_g[None], min(ROW_TILE, tp)).reshape(bp, tp, D_MODEL)
    y_sample = _final_norm(xs, final_norm_g[None], min(ROW_TILE, bs * ts)).reshape(bs, ts, D_MODEL)
    stk = lambda sts, i: jnp.stack([s[i] for s in sts], axis=0)
    return (y_prompt, y_sample,
            stk(st_p, 0), stk(st_p, 1), stk(st_p, 2), stk(st_p, 3), stk(st_p, 4), stk(st_p, 5),
            stk(st_s, 0), stk(st_s, 1), stk(st_s, 2), stk(st_s, 3), stk(st_s, 4), stk(st_s, 5))
```

```python
import functools
import math
from typing import NamedTuple

import numpy as np
import jax
import jax.numpy as jnp
from jax import lax
from jax.experimental import pallas as pl
from jax.experimental.pallas import tpu as pltpu

F32 = jnp.float32
BF = jnp.bfloat16

D_MODEL = 1024
DEPTH = 4
PAGE_SIZE = 128
HEAD_DIM = 64
GW = 256
NH = 4
HG_CHUNK = 64
HG_CHUNKS_PER_STEP = 4
MOBA_BLOCK = 256
MOBA_TOPK = 3
ROT_DIM = 16
ROPE_THETA = 500000.0
POOL_WINDOWS = (2, 4, 8, 16)
POOL_BUF = 15
N_GROUPS = 4
EXP_PER_GROUP = 4
N_EXPERTS = 16
D_EXPERT = 512
IN_COLS = 11 * GW
EPS = 1e-6
NEG_BIG = -1e30
F_FLOOR = 1e-30
SCALE = 1.0 / math.sqrt(HEAD_DIM)

V7X_VMEM_BYTES = 64 * 1024 * 1024
VMEM_LIMIT = V7X_VMEM_BYTES - 12 * 1024 * 1024
LANES = 128
ROW_TILE = 512


def _params():
    return pltpu.CompilerParams(vmem_limit_bytes=VMEM_LIMIT)


def _mm(a, b):
    return jnp.dot(a, b, preferred_element_type=F32)


def _mm_nt(a, b):
    return lax.dot_general(a, b, (((1,), (1,)), ((), ())), preferred_element_type=F32)


def _mm_tn(a, b):
    return lax.dot_general(a, b, (((0,), (0,)), ((), ())), preferred_element_type=F32)


def _split2(x):
    hi = x.astype(BF)
    lo = (x - hi.astype(F32)).astype(BF)
    return hi, lo


def _split3(x):
    hi = x.astype(BF)
    r = x - hi.astype(F32)
    mid = r.astype(BF)
    lo = (r - mid.astype(F32)).astype(BF)
    return hi, mid, lo


def _mm_exact_lhs(w01, x):
    hi, mid, lo = _split3(x)
    return _mm(w01, hi) + _mm(w01, mid) + _mm(w01, lo)


def _mm3_nt(a, b):
    ah, al = _split2(a)
    bh, bl = _split2(b)
    return _mm_nt(ah, bh) + _mm_nt(ah, bl) + _mm_nt(al, bh)


def _mm3(a, b):
    ah, al = _split2(a)
    bh, bl = _split2(b)
    return _mm(ah, bh) + _mm(ah, bl) + _mm(al, bh)


def _sigmoid(x):
    return 1.0 / (1.0 + jnp.exp(-x))


def _silu(x):
    return x * _sigmoid(x)


def _log_sigmoid_pair(z):
    t = jnp.log(1.0 + jnp.exp(-jnp.abs(z)))
    return jnp.minimum(z, 0.0) - t, -jnp.maximum(z, 0.0) - t


def _ada_kernel(c_ref, w_ref, b_ref, o_ref):
    o_ref[0] = _mm(c_ref[...].astype(BF), w_ref[0].astype(BF)) + b_ref[0]


def _ada_all(c_all, ada_w, ada_b):
    m = c_all.shape[0]
    tn = 1536
    return pl.pallas_call(
        _ada_kernel,
        out_shape=jax.ShapeDtypeStruct((DEPTH, m, 6 * D_MODEL), F32),
        grid=(DEPTH, 6 * D_MODEL // tn),
        in_specs=[
            pl.BlockSpec((m, D_MODEL), lambda l, j: (0, 0)),
            pl.BlockSpec((1, D_MODEL, tn), lambda l, j: (l, 0, j)),
            pl.BlockSpec((1, 1, tn), lambda l, j: (l, 0, j)),
        ],
        out_specs=pl.BlockSpec((1, m, tn), lambda l, j: (l, 0, j)),
        compiler_params=_params(),
        name="ada_mod",
    )(c_all, ada_w, ada_b.reshape(DEPTH, 1, 6 * D_MODEL))


class _Mod(NamedTuple):
    arr: jax.Array
    per_seq: bool
    layer: int
    kind: int
    nseq: int


def _mod_spec(mod, tm, rows_per_group):
    if mod.per_seq:
        tiles = rows_per_group // tm
        base = (mod.layer * mod.nseq) * 6 + mod.kind
        return pl.BlockSpec((1, 1, D_MODEL), lambda i, *_: (base + (i // tiles) * 6, 0, 0))
    return pl.BlockSpec((1, tm, D_MODEL), lambda i, *_: (mod.layer, i, mod.kind))


def _mod_val(ref):
    return ref[0]


def _rms_mod(x, g, sc, sh):
    ms = jnp.mean(x * x, axis=-1, keepdims=True)
    return (x * lax.rsqrt(ms + EPS) * g) * (1.0 + sc) + sh


def _inproj_kernel(x_ref, g_ref, sc_ref, sh_ref, w_ref, u_ref, u16_ref, *ut_ref):
    h = _rms_mod(x_ref[...], g_ref[...], _mod_val(sc_ref), _mod_val(sh_ref))
    r = _mm(h.astype(BF), w_ref[...])
    u_ref[...] = r
    kv = r[:, 6 * GW:10 * GW]
    u16_ref[...] = kv.astype(BF)
    if ut_ref:
        ut_ref[0][0] = jnp.transpose(kv)


def _inproj(x, g, sc, sh, w16, rows_per_group, tm, transposed):
    n = x.shape[0]
    nt = rows_per_group // tm
    out_shape = [jax.ShapeDtypeStruct((n, IN_COLS), F32), jax.ShapeDtypeStruct((n, 4 * GW), BF)]
    out_specs = [pl.BlockSpec((tm, IN_COLS), lambda i: (i, 0)), pl.BlockSpec((tm, 4 * GW), lambda i: (i, 0))]
    if transposed:
        out_shape.append(jax.ShapeDtypeStruct((n // rows_per_group, 4 * GW, rows_per_group), F32))
        out_specs.append(pl.BlockSpec((1, 4 * GW, tm), lambda i: (i // nt, 0, i % nt)))
    return pl.pallas_call(
        _inproj_kernel,
        out_shape=tuple(out_shape),
        grid=(n // tm,),
        in_specs=[
            pl.BlockSpec((tm, D_MODEL), lambda i: (i, 0)),
            pl.BlockSpec((1, D_MODEL), lambda i: (0, 0)),
            _mod_spec(sc, tm, rows_per_group),
            _mod_spec(sh, tm, rows_per_group),
            pl.BlockSpec((D_MODEL, IN_COLS), lambda i: (0, 0)),
        ],
        out_specs=tuple(out_specs),
        compiler_params=_params(),
        name="in_proj",
    )(x, g, sc.arr, sh.arr, w16)


def _hgrn_consts(c):
    r = np.arange(c)
    mats = [(r[None, :] <= r[:, None]), (r[None, :] > r[:, None])]
    masks = []
    w = c // 2
    while w >= 1:
        blk = r // (2 * w)
        half = (r // w) % 2
        mid = blk * 2 * w + w - 1
        t_rows = (r[None, :] > mid[:, None]) & (r[None, :] <= r[:, None])
        s_rows = (r[None, :] > r[:, None]) & (r[None, :] <= mid[:, None])
        mats.append(np.where(half[:, None] == 1, t_rows, s_rows))
        masks.append((blk[:, None] == blk[None, :]) & (half[:, None] == 1) & (half[None, :] == 0))
        w //= 2
    masks.append(r[:, None] == r[None, :])
    wmat = np.concatenate(mats, axis=0).astype(np.float32)
    return jnp.asarray(wmat, BF), jnp.asarray(np.stack(masks).astype(np.float32))


def _hgrn_kernel(q_ref, f_ref, i_ref, g_ref, lb_ref, hg_ref, w_ref, mk_ref, bd_ref, s0_ref,
                 o_ref, st_ref, *, groups, chunk, cps, nlev):
    c = chunk

    @pl.when(pl.program_id(1) == 0)
    def _():
        st_ref[...] = s0_ref[...]

    lbb = jnp.clip(lb_ref[...], 0.0, 1.0)
    wmat = w_ref[...]
    for g in range(groups):
        a = f_ref[g]
        q_all = q_ref[g]
        iv16_all = i_ref[g].astype(BF)
        f = lbb + (1.0 - lbb) * _sigmoid(a)
        lf = jnp.log(jnp.maximum(f, F_FLOOR))
        kk_all = (1.0 - lbb) * _sigmoid(-a)
        lf_cat = jnp.concatenate([lf[cc * c:(cc + 1) * c] for cc in range(cps)], axis=1)
        e_cat = _mm_exact_lhs(wmat, lf_cat)
        st = st_ref[g]
        outs = []
        for cc in range(cps):
            e = e_cat[:, cc * GW:(cc + 1) * GW]
            q = q_all[cc * c:(cc + 1) * c]
            kk = kk_all[cc * c:(cc + 1) * c]
            iv16 = iv16_all[cc * c:(cc + 1) * c]
            b = e[0:c]
            qb16 = (q * jnp.exp(b)).astype(BF)
            kr16 = (kk * jnp.exp(e[c:2 * c])).astype(BF)
            dec_last = jnp.exp(b[c - 1:c])
            qt = [(q * jnp.exp(e[(2 + l) * c:(3 + l) * c])).astype(BF) for l in range(nlev)]
            kt = [(kk * jnp.exp(e[(2 + l) * c:(3 + l) * c])).astype(BF) for l in range(nlev)]
            qt.append(q.astype(BF))
            kt.append(kk.astype(BF))
            st16 = st.astype(BF)
            o_heads, st_heads = [], []
            for h in range(NH):
                sl = slice(h * HEAD_DIM, (h + 1) * HEAD_DIM)
                att = jnp.zeros((c, c), F32)
                for l in range(nlev + 1):
                    att = att + mk_ref[l] * _mm_nt(qt[l][:, sl], kt[l][:, sl])
                o_heads.append(_mm(att.astype(BF), iv16[:, sl]) + _mm_nt(qb16[:, sl], st16[:, sl]))
                st_heads.append(st[:, sl] * dec_last[:, sl] + _mm_tn(iv16[:, sl], kr16[:, sl]))
            outs.append(jnp.concatenate(o_heads, axis=1))
            st = jnp.concatenate(st_heads, axis=1)
        st_ref[g] = st
        o = jnp.concatenate(outs, axis=0)
        oh, ol = _split2(o * o)
        ms = _mm(oh, bd_ref[...]) + _mm(ol, bd_ref[...])
        o_ref[g] = o * lax.rsqrt(ms + EPS) * hg_ref[...] * _silu(g_ref[g])


def _hgrn(u3, lb, hg, s0t, chunk, groups, cps):
    bsz, t, _ = u3.shape
    nlev = int(math.log2(chunk))
    wmat, masks = _hgrn_consts(chunk)
    bd = jnp.asarray(np.kron(np.eye(NH), np.full((HEAD_DIM, HEAD_DIM), 1.0 / HEAD_DIM)), BF)
    rows = chunk * cps
    col = lambda k: pl.BlockSpec((groups, rows, GW), lambda b, c: (b, c, k))
    const = lambda shape: pl.BlockSpec(shape, lambda b, c: (0,) * len(shape))
    return pl.pallas_call(
        functools.partial(_hgrn_kernel, groups=groups, chunk=chunk, cps=cps, nlev=nlev),
        out_shape=(jax.ShapeDtypeStruct((bsz, t, GW), F32),
                   jax.ShapeDtypeStruct((bsz, HEAD_DIM, GW), F32)),
        grid=(bsz // groups, t // rows),
        in_specs=[col(0), col(1), col(2), col(3), const((1, GW)), const((1, GW)),
                  const(wmat.shape), const(masks.shape), const((GW, GW)),
                  pl.BlockSpec((groups, HEAD_DIM, GW), lambda b, c: (b, 0, 0))],
        out_specs=(pl.BlockSpec((groups, rows, GW), lambda b, c: (b, c, 0)),
                   pl.BlockSpec((groups, HEAD_DIM, GW), lambda b, c: (b, 0, 0))),
        compiler_params=_params(),
        name="hgrn2",
    )(u3, u3, u3, u3, lb, hg, wmat, masks, bd, s0t)


def _rope_tables(pos):
    half = ROT_DIM // 2
    inv = np.exp(-math.log(ROPE_THETA) * np.arange(half, dtype=np.float32) * 2.0 / ROT_DIM)
    ang = jnp.asarray(pos, F32)[:, None] * jnp.asarray(inv)[None, :]
    cos, sin = jnp.cos(ang), jnp.sin(ang)
    n = ang.shape[0]
    ones = jnp.ones((n, HEAD_DIM - ROT_DIM), F32)
    zeros = jnp.zeros((n, HEAD_DIM - ROT_DIM), F32)
    ch = jnp.concatenate([cos, cos, ones], axis=1)
    sh = jnp.concatenate([-sin, sin, zeros], axis=1)
    return jnp.tile(ch, (1, NH)), jnp.tile(sh, (1, NH))


def _rope_kernel(q_ref, k_ref, cos_ref, sin_ref, qo_ref, ko_ref, k16_ref, km_ref):
    cos = cos_ref[...]
    sin = sin_ref[...]
    lane = lax.broadcasted_iota(jnp.int32, cos.shape, 1) % HEAD_DIM
    first = lane < ROT_DIM // 2

    def rot(x):
        partner = jnp.where(first, pltpu.roll(x, GW - ROT_DIM // 2, 1), pltpu.roll(x, ROT_DIM // 2, 1))
        return x * cos + partner * sin

    qo_ref[...] = rot(q_ref[...])
    k = rot(k_ref[...])
    if len(ko_ref.shape) == 3:
        ko_ref[0] = jnp.transpose(k)
    else:
        ko_ref[...] = k
    k16_ref[...] = k.astype(BF)
    km_ref[0] = jnp.mean(k, axis=0, keepdims=True)


def _rope(u, cos, sin, nbatch, rows, transposed):
    n = u.shape[0]
    t = n // nbatch
    nt = t // rows
    if transposed:
        k_shape = jax.ShapeDtypeStruct((nbatch, GW, t), F32)
        k_spec = pl.BlockSpec((1, GW, rows), lambda b, j: (b, 0, j))
    else:
        k_shape = jax.ShapeDtypeStruct((n, GW), F32)
        k_spec = pl.BlockSpec((rows, GW), lambda b, j: (b * nt + j, 0))
    return pl.pallas_call(
        _rope_kernel,
        out_shape=(jax.ShapeDtypeStruct((n, GW), F32), k_shape,
                   jax.ShapeDtypeStruct((n, GW), BF), jax.ShapeDtypeStruct((n // rows, 1, GW), F32)),
        grid=(nbatch, nt),
        in_specs=[pl.BlockSpec((rows, GW), lambda b, j: (b * nt + j, 4)),
                  pl.BlockSpec((rows, GW), lambda b, j: (b * nt + j, 5)),
                  pl.BlockSpec((rows, GW), lambda b, j: (j, 0)),
                  pl.BlockSpec((rows, GW), lambda b, j: (j, 0))],
        out_specs=(pl.BlockSpec((rows, GW), lambda b, j: (b * nt + j, 0)),
                   k_spec,
                   pl.BlockSpec((rows, GW), lambda b, j: (b * nt + j, 0)),
                   pl.BlockSpec((1, 1, GW), lambda b, j: (b * nt + j, 0, 0))),
        compiler_params=_params(),
        name="rope",
    )(u, u, cos, sin)


def _topk_rows(scores, valid, idx):
    cur = jnp.where(valid, scores, NEG_BIG)
    picked = jnp.zeros(scores.shape, F32)
    big = jnp.int32(1 << 30)
    for _ in range(MOBA_TOPK):
        mx = jnp.max(cur, axis=0, keepdims=True)
        first = jnp.min(jnp.where(cur == mx, idx, big), axis=0, keepdims=True)
        pick = idx == first
        picked = jnp.where(pick & valid, 1.0, picked)
        cur = jnp.where(pick, -jnp.inf, cur)
    return picked


def _pair_weights(qt16, tq):
    z = jnp.zeros((HEAD_DIM, tq), qt16.dtype)
    out = []
    for p in range(NH // 2):
        a = qt16[(2 * p) * HEAD_DIM:(2 * p + 1) * HEAD_DIM]
        b = qt16[(2 * p + 1) * HEAD_DIM:(2 * p + 2) * HEAD_DIM]
        out.append(jnp.concatenate([jnp.concatenate([a, z], axis=1), jnp.concatenate([z, b], axis=1)], axis=0))
    return out


def _moba_kernel(q_ref, k_ref, vt_ref, km_ref, o_ref, m_ref, l_ref, acc_ref, sel_ref, *, nb):
    own = pl.program_id(1)
    q = q_ref[...]
    tq = q.shape[0]
    qt = jnp.transpose(q)
    rowi = lax.broadcasted_iota(jnp.int32, (nb, tq), 0)
    km = km_ref[...]
    for h in range(NH):
        sl = slice(h * HEAD_DIM, (h + 1) * HEAD_DIM)
        sel_ref[h] = _topk_rows(_mm3(km[:, sl], qt[sl, :]), rowi < own, rowi)
    qp = _pair_weights((qt * SCALE).astype(BF), tq)
    m_ref[...] = jnp.full(m_ref.shape, -jnp.inf, F32)
    l_ref[...] = jnp.zeros(l_ref.shape, F32)
    acc_ref[...] = jnp.zeros(acc_ref.shape, F32)

    def step(start, nkeys, keep_fn):
        kblk = k_ref[pl.ds(start, nkeys), :]
        for p in range(NH // 2):
            sp = _mm(kblk[:, p * 2 * HEAD_DIM:(p + 1) * 2 * HEAD_DIM], qp[p])
            for r in range(2):
                h = 2 * p + r
                rows = slice(h * HEAD_DIM, (h + 1) * HEAD_DIM)
                s = jnp.where(keep_fn(h), sp[:, r * tq:(r + 1) * tq], NEG_BIG)
                m_old = m_ref[h, 0:1, :]
                m_new = jnp.maximum(m_old, jnp.max(s, axis=0, keepdims=True))
                pt = jnp.exp(s - m_new)
                alpha = jnp.exp(m_old - m_new)
                l_ref[h, 0:1, :] = alpha * l_ref[h, 0:1, :] + jnp.sum(pt, axis=0, keepdims=True)
                vt = vt_ref[0, rows, pl.ds(start, nkeys)].astype(BF)
                acc_ref[rows, :] = alpha * acc_ref[rows, :] + _mm(vt, pt.astype(BF))
                m_ref[h, 0:1, :] = m_new

    def selected(h, n):
        return jnp.broadcast_to(sel_ref[h, pl.ds(n, 1), :], (MOBA_BLOCK, tq))

    def past_pair(i, carry):
        n = 2 * i
        step(pl.multiple_of(n * MOBA_BLOCK, 2 * MOBA_BLOCK), 2 * MOBA_BLOCK,
             lambda h: jnp.concatenate([selected(h, n), selected(h, n + 1)], axis=0) > 0.5)
        return carry

    lax.fori_loop(0, own // 2, past_pair, 0)

    @pl.when(own % 2 == 1)
    def _():
        step(pl.multiple_of((own - 1) * MOBA_BLOCK, MOBA_BLOCK), MOBA_BLOCK, lambda h: selected(h, own - 1) > 0.5)

    keyi = lax.broadcasted_iota(jnp.int32, (MOBA_BLOCK, tq), 0)
    qi = lax.broadcasted_iota(jnp.int32, (MOBA_BLOCK, tq), 1)
    step(pl.multiple_of(own * MOBA_BLOCK, MOBA_BLOCK), MOBA_BLOCK, lambda h: keyi <= qi)
    out_t = jnp.concatenate(
        [acc_ref[h * HEAD_DIM:(h + 1) * HEAD_DIM, :] / l_ref[h, 0:1, :] for h in range(NH)], axis=0)
    o_ref[...] = jnp.transpose(out_t)


def _moba_prompt(qrot, k16, vt16, kmean, nbatch):
    n = qrot.shape[0]
    t = n // nbatch
    nq = t // MOBA_BLOCK
    nb = t // MOBA_BLOCK
    return pl.pallas_call(
        functools.partial(_moba_kernel, nb=nb),
        out_shape=jax.ShapeDtypeStruct((n, GW), F32),
        grid=(nbatch, nq),
        in_specs=[pl.BlockSpec((MOBA_BLOCK, GW), lambda b, j: (b * nq + j, 0)),
                  pl.BlockSpec((t, GW), lambda b, j: (b, 0)),
                  pl.BlockSpec((1, GW, t), lambda b, j: (b, 0, 0)),
                  pl.BlockSpec((nb, GW), lambda b, j: (b, 0))],
        out_specs=pl.BlockSpec((MOBA_BLOCK, GW), lambda b, j: (b * nq + j, 0)),
        scratch_shapes=[pltpu.VMEM((NH, 8, MOBA_BLOCK), F32), pltpu.VMEM((NH, 8, MOBA_BLOCK), F32),
                        pltpu.VMEM((GW, MOBA_BLOCK), F32), pltpu.VMEM((NH, nb, MOBA_BLOCK), F32)],
        compiler_params=_params(),
        name="moba_prompt",
    )(qrot, k16, vt16, kmean)


SB_BLOCK = 256
SB_KEY_STEP = 256
SB_CUTOFF = 104.0


def _later_mask(n, rows_are_keys):
    r = np.arange(n)
    m = r[None, :] > r[:, None]
    return jnp.asarray((m if rows_are_keys else m.T).astype(np.float32), BF)


def _sb_kernel(q_ref, k_ref, vt_ref, up_ref, o_ref, carry_ref, acc_ref):
    j = pl.program_id(1)
    tq = q_ref.shape[0]
    qp = _pair_weights((jnp.transpose(q_ref[...]) * SCALE).astype(BF), tq)
    lower = up_ref[...]
    carry_ref[...] = jnp.zeros(carry_ref.shape, F32)
    acc_ref[...] = jnp.zeros(acc_ref.shape, F32)

    def block(start, mask):
        kblk = k_ref[pl.ds(start, SB_KEY_STEP), :]
        for p in range(NH // 2):
            zp = _mm(kblk[:, p * 2 * HEAD_DIM:(p + 1) * 2 * HEAD_DIM], qp[p])
            for r in range(2):
                h = 2 * p + r
                rows = slice(h * HEAD_DIM, (h + 1) * HEAD_DIM)
                lpos, lneg = _log_sigmoid_pair(zp[:, r * tq:(r + 1) * tq])
                if mask is not None:
                    lneg = jnp.where(mask, lneg, 0.0)
                hi, lo = _split2(lneg)
                rest = carry_ref[h:h + 1, :] + _mm(lower, hi) + _mm(lower, lo)
                a = jnp.exp(lpos + jnp.minimum(rest, 0.0))
                if mask is not None:
                    a = jnp.where(mask, a, 0.0)
                vt = vt_ref[0, rows, pl.ds(start, SB_KEY_STEP)].astype(BF)
                acc_ref[rows, :] = acc_ref[rows, :] + _mm(vt, a.astype(BF))
                carry_ref[h:h + 1, :] = carry_ref[h:h + 1, :] + jnp.sum(lneg, axis=0, keepdims=True)

    def active():
        return (jnp.max(carry_ref[0:NH, :]) > -SB_CUTOFF).astype(jnp.int32)

    keyi = lax.broadcasted_iota(jnp.int32, (SB_KEY_STEP, tq), 0)
    qi = lax.broadcasted_iota(jnp.int32, (SB_KEY_STEP, tq), 1)
    steps_per_tile = SB_BLOCK // SB_KEY_STEP
    for d in range(steps_per_tile - 1, -1, -1):
        block(pl.multiple_of(j * SB_BLOCK + d * SB_KEY_STEP, SB_KEY_STEP), keyi + d * SB_KEY_STEP < qi)

    def body(c):
        block(pl.multiple_of(c[0] * SB_KEY_STEP, SB_KEY_STEP), None)
        return c[0] - 1, active()

    lax.while_loop(lambda c: (c[0] >= 0) & (c[1] > 0), body, (j * steps_per_tile - 1, active()))
    o_ref[...] = jnp.transpose(acc_ref[...])


def _sb_prompt(u, u16, vt16, nbatch):
    n = u.shape[0]
    t = n // nbatch
    nq = t // SB_BLOCK
    return pl.pallas_call(
        _sb_kernel,
        out_shape=jax.ShapeDtypeStruct((n, GW), F32),
        grid=(nbatch, nq),
        in_specs=[pl.BlockSpec((SB_BLOCK, GW), lambda b, j: (b * nq + j, 7)),
                  pl.BlockSpec((t, GW), lambda b, j: (b, 2)),
                  pl.BlockSpec((1, GW, t), lambda b, j: (b, 3, 0)),
                  pl.BlockSpec((SB_KEY_STEP, SB_KEY_STEP), lambda b, j: (0, 0))],
        out_specs=pl.BlockSpec((SB_BLOCK, GW), lambda b, j: (b * nq + j, 0)),
        scratch_shapes=[pltpu.VMEM((8, SB_BLOCK), F32), pltpu.VMEM((GW, SB_BLOCK), F32)],
        compiler_params=_params(),
        name="sb_prompt",
    )(u, u16, vt16, _later_mask(SB_KEY_STEP, True))


def _block_diag_q(q, nbatch, t):
    q4 = q.reshape(nbatch, t, NH, HEAD_DIM)
    eye = jnp.eye(NH, dtype=q.dtype)
    bd = q4[:, None, :, :, :] * eye[None, :, None, :, None]
    return bd.reshape(nbatch, NH * t, GW)


def _page_specs(layer, n_phys, n_pages):
    def spec(p):
        return pl.BlockSpec((1, GW, PAGE_SIZE), lambda b, pt: (layer * n_phys + pt[b * n_pages + p], 0, 0))
    return [spec(p) for p in range(n_pages)]


def _heads_out(acc, den, t):
    return jnp.concatenate(
        [acc[h * t:(h + 1) * t, h * HEAD_DIM:(h + 1) * HEAD_DIM] / den[h * t:(h + 1) * t]
         for h in range(NH)], axis=1)


def _pad_rows(x, rows):
    return jnp.concatenate([x, jnp.zeros((rows - x.shape[0], x.shape[1]), x.dtype)], axis=0)


def _moba_dec_kernel(pt_ref, qbd_ref, knew_ref, vnew_ref, *refs, n_pages, t, own):
    del pt_ref
    kt_refs, vt_refs, o_ref = refs[:n_pages], refs[n_pages:2 * n_pages], refs[2 * n_pages]
    qbd = qbd_ref[0]
    nrow = qbd.shape[0]
    ppb = MOBA_BLOCK // PAGE_SIZE
    lg = [_mm3(qbd, kt_refs[p][0]) for p in range(n_pages)]
    score = []
    for n in range(own):
        tot = jnp.sum(lg[n * ppb], axis=1, keepdims=True)
        for r in range(1, ppb):
            tot = tot + jnp.sum(lg[n * ppb + r], axis=1, keepdims=True)
        score.append(tot * (1.0 / MOBA_BLOCK))
    sel = []
    for n in range(own):
        rank = jnp.zeros((nrow, 1), F32)
        for m in range(own):
            if m != n:
                beats = (score[m] >= score[n]) if m < n else (score[m] > score[n])
                rank = rank + jnp.where(beats, 1.0, 0.0)
        sel.append(rank < float(MOBA_TOPK))
    logits = [jnp.where(sel[p // ppb], lg[p] * SCALE, NEG_BIG) for p in range(n_pages)]
    lnew = _mm3_nt(qbd, _pad_rows(knew_ref[...], PAGE_SIZE)) * SCALE
    keyi = lax.broadcasted_iota(jnp.int32, lnew.shape, 1)
    qi = lax.broadcasted_iota(jnp.int32, lnew.shape, 0) % t
    lnew = jnp.where(keyi <= qi, lnew, NEG_BIG)
    m = jnp.max(lnew, axis=1, keepdims=True)
    for lp in logits:
        m = jnp.maximum(m, jnp.max(lp, axis=1, keepdims=True))
    enew = jnp.exp(lnew - m)
    den = jnp.sum(enew, axis=1, keepdims=True)
    acc = _mm(enew.astype(BF), _pad_rows(vnew_ref[...], PAGE_SIZE).astype(BF))
    for p in range(n_pages):
        e = jnp.exp(logits[p] - m)
        den = den + jnp.sum(e, axis=1, keepdims=True)
        acc = acc + _mm_nt(e.astype(BF), vt_refs[p][0].astype(BF))
    o_ref[...] = _heads_out(acc, den, t)


def _sb_dec_kernel(pt_ref, qbd_ref, knew_ref, vnew_ref, later_ref, *refs, n_pages, t):
    del pt_ref
    kt_refs, vt_refs, o_ref = refs[:n_pages], refs[n_pages:2 * n_pages], refs[2 * n_pages]
    qbd16 = (qbd_ref[0] * SCALE).astype(BF)
    nrow = qbd16.shape[0]
    later = later_ref[...]

    def block(z, carry, mask):
        lpos, lneg = _log_sigmoid_pair(z)
        if mask is not None:
            lneg = jnp.where(mask, lneg, 0.0)
        hi, lo = _split2(lneg)
        rest = carry + _mm(hi, later) + _mm(lo, later)
        a = jnp.exp(lpos + jnp.minimum(rest, 0.0))
        if mask is not None:
            a = jnp.where(mask, a, 0.0)
        return a.astype(BF), carry + jnp.sum(lneg, axis=1, keepdims=True)

    keyi = lax.broadcasted_iota(jnp.int32, (nrow, PAGE_SIZE), 1)
    qi = lax.broadcasted_iota(jnp.int32, (nrow, PAGE_SIZE), 0) % t
    znew = _mm_nt(qbd16, _pad_rows(knew_ref[...], PAGE_SIZE).astype(BF))
    a, carry = block(znew, jnp.zeros((nrow, 1), F32), keyi < qi)
    acc = _mm(a, _pad_rows(vnew_ref[...], PAGE_SIZE).astype(BF))
    for p in range(n_pages - 1, -1, -1):
        def visit(state, p=p):
            a, carry_new = block(_mm(qbd16, kt_refs[p][0].astype(BF)), state[1], None)
            return state[0] + _mm_nt(a, vt_refs[p][0].astype(BF)), carry_new
        acc, carry = lax.cond(jnp.max(carry) > -SB_CUTOFF, visit, lambda state: state, (acc, carry))
    o_ref[...] = _heads_out(acc, jnp.ones((nrow, 1), F32), t)


def _dec_attention(kind, layer, page_table, qbd, knew, vnew, kcache, vcache, t, pos0):
    nbatch, n_pages = page_table.shape
    n_phys = kcache.shape[0] // DEPTH
    assert pos0 == n_pages * PAGE_SIZE and pos0 % MOBA_BLOCK == 0 and t <= PAGE_SIZE
    pspecs = _page_specs(layer, n_phys, n_pages)
    row = pl.BlockSpec((t, GW), lambda b, pt: (b, 0))
    in_specs = [pl.BlockSpec((1, NH * t, GW), lambda b, pt: (b, 0, 0)), row, row]
    args = [qbd, knew, vnew]
    if kind == "moba":
        body = functools.partial(_moba_dec_kernel, n_pages=n_pages, t=t, own=pos0 // MOBA_BLOCK)
    else:
        body = functools.partial(_sb_dec_kernel, n_pages=n_pages, t=t)
        in_specs.append(pl.BlockSpec((PAGE_SIZE, PAGE_SIZE), lambda b, pt: (0, 0)))
        args.append(_later_mask(PAGE_SIZE, False))
    in_specs += pspecs + pspecs
    args += [kcache] * n_pages + [vcache] * n_pages
    return pl.pallas_call(
        body,
        out_shape=jax.ShapeDtypeStruct((nbatch * t, GW), F32),
        grid_spec=pltpu.PrefetchScalarGridSpec(
            num_scalar_prefetch=1, grid=(nbatch,), in_specs=in_specs,
            out_specs=pl.BlockSpec((t, GW), lambda b, pt: (b, 0))),
        compiler_params=_params(),
        name=kind + "_decode",
    )(page_table.reshape(-1), *args)


POOL_HALO = 16


def _pool_bands(seqs, t):
    halo_seq = np.repeat(np.arange(seqs), POOL_HALO)
    halo_e = np.tile(np.arange(POOL_HALO), seqs)
    row_seq = np.repeat(np.arange(seqs), t)
    row_e = np.tile(np.arange(t), seqs) + POOL_HALO
    col_seq = np.concatenate([halo_seq, row_seq])[None, :]
    col_e = np.concatenate([halo_e, row_e])[None, :]
    same = col_seq == row_seq[:, None]
    r = row_e[:, None]
    return jnp.asarray(np.stack([(same & (col_e <= r) & (col_e > r - w)).astype(np.float32)
                                 for w in POOL_WINDOWS]), BF)


def _pool_kernel(u_ref, halo_ref, buf_ref, band_ref, pw_ref, ps_ref, o_ref, *, rows, nt, t_seq, pos0):
    i = pl.program_id(0)
    u = u_ref[...]
    halo = jnp.where(i % nt == 0, buf_ref[...], halo_ref[...])
    ext = jnp.concatenate([halo, u], axis=0)
    hi, mid, lo = _split3(ext)
    tok = (i * rows + lax.broadcasted_iota(jnp.int32, (rows, 1), 0)) % t_seq
    pos = (pos0 + tok).astype(F32)
    parts = []
    gc = GW // len(POOL_WINDOWS)
    for gi, w in enumerate(POOL_WINDOWS):
        sl = slice(gi * gc, (gi + 1) * gc)
        band = band_ref[gi]
        ssum = _mm(band, hi[:, sl]) + _mm(band, mid[:, sl]) + _mm(band, lo[:, sl])
        cnt = jnp.minimum(float(w), pos + 1.0)
        parts.append(ssum / cnt - u[:, sl])
    d = jnp.concatenate(parts, axis=1)
    o_ref[...] = _mm(d.astype(BF), pw_ref[...]) * ps_ref[...]


def _pool(u, buf_flat, pw_bd16, ps, nbatch, t, pos0):
    n = u.shape[0]
    if t >= MOBA_BLOCK:
        rows, seqs, hr = 256, 1, POOL_HALO
        nt = t // rows
        band = _pool_bands(1, rows)
        halo_arr = u
        halo_spec = pl.BlockSpec((hr, GW), lambda i: (jnp.maximum(i * (rows // hr) - 1, 0), 10))
        buf_spec = pl.BlockSpec((hr, GW), lambda i: (i // nt, 0))
    else:
        seqs = 128 // t
        rows, hr, nt = seqs * t, seqs * POOL_HALO, 1
        band = _pool_bands(seqs, t)
        halo_arr = buf_flat
        halo_spec = pl.BlockSpec((hr, GW), lambda i: (i, 0))
        buf_spec = halo_spec
    return pl.pallas_call(
        functools.partial(_pool_kernel, rows=rows, nt=nt, t_seq=t, pos0=pos0),
        out_shape=jax.ShapeDtypeStruct((n, GW), F32),
        grid=(n // rows,),
        in_specs=[pl.BlockSpec((rows, GW), lambda i: (i, 10)), halo_spec, buf_spec,
                  pl.BlockSpec(band.shape, lambda i: (0, 0, 0)),
                  pl.BlockSpec((GW, GW), lambda i: (0, 0)),
                  pl.BlockSpec((1, GW), lambda i: (0, 0))],
        out_specs=pl.BlockSpec((rows, GW), lambda i: (i, 0)),
        compiler_params=_params(),
        name="pool_mixer",
    )(u, halo_arr, buf_flat, band, pw_bd16, ps)


def _route(logits):
    lane = lax.broadcasted_iota(jnp.int32, logits.shape, 1)
    big = jnp.int32(1 << 30)
    gmask = lane < N_GROUPS
    gl = jnp.where(gmask, logits, -jnp.inf)
    gmax = jnp.max(gl, axis=-1, keepdims=True)
    gsum = jnp.sum(jnp.where(gmask, jnp.exp(gl - gmax), 0.0), axis=-1, keepdims=True)
    g_top = 1.0 / gsum
    g_idx = jnp.min(jnp.where(gl == gmax, lane, big), axis=-1, keepdims=True)
    emask = (lane >= N_GROUPS) & (lane < N_GROUPS + N_EXPERTS) & ((lane - N_GROUPS) // EXP_PER_GROUP == g_idx)
    el = jnp.where(emask, logits, -jnp.inf)
    m1 = jnp.max(el, axis=-1, keepdims=True)
    i1 = jnp.min(jnp.where(el == m1, lane, big), axis=-1, keepdims=True)
    el2 = jnp.where(lane == i1, -jnp.inf, el)
    m2 = jnp.max(el2, axis=-1, keepdims=True)
    i2 = jnp.min(jnp.where(el2 == m2, lane, big), axis=-1, keepdims=True)
    r = jnp.exp(m2 - m1)
    w1 = g_top / (1.0 + r)
    w2 = g_top * r / (1.0 + r)
    gate = jnp.where(lane == i1, w1, jnp.where(lane == i2, w2, 0.0))
    info = jnp.where(lane == 0, (i1 - N_GROUPS).astype(F32),
                     jnp.where(lane == 1, (i2 - N_GROUPS).astype(F32),
                               jnp.where(lane == 2, w1, jnp.where(lane == 3, w2, 0.0))))
    return gate, info


def _outproj_kernel(oa_ref, ob_ref, oc_ref, od_ref, x_ref, gt_ref, g_ref, sc_ref, sh_ref, w_ref,
                    wr_ref, br_ref, x1_ref, h_ref, gate_ref, info_ref):
    merged = jnp.concatenate([oa_ref[...], ob_ref[...], oc_ref[...], od_ref[...]], axis=1).astype(BF)
    x1 = x_ref[...] + _mod_val(gt_ref) * _mm(merged, w_ref[...])
    x1_ref[...] = x1
    h = _rms_mod(x1, g_ref[...], _mod_val(sc_ref), _mod_val(sh_ref))
    h_ref[...] = h.astype(h_ref.dtype)
    gate_ref[...], info_ref[...] = _route(_mm3(h, wr_ref[...]) + br_ref[...])


def _outproj(oa, ob, oc, od, x, gt, g, sc, sh, w16, wr, br, rows_per_group, tm, h_dtype):
    n = x.shape[0]
    quarter = pl.BlockSpec((tm, GW), lambda i: (i, 0))
    full = pl.BlockSpec((tm, D_MODEL), lambda i: (i, 0))
    narrow = pl.BlockSpec((tm, LANES), lambda i: (i, 0))
    return pl.pallas_call(
        _outproj_kernel,
        out_shape=(jax.ShapeDtypeStruct((n, D_MODEL), F32), jax.ShapeDtypeStruct((n, D_MODEL), h_dtype),
                   jax.ShapeDtypeStruct((n, LANES), F32), jax.ShapeDtypeStruct((n, LANES), F32)),
        grid=(n // tm,),
        in_specs=[quarter, quarter, quarter, quarter, full,
                  _mod_spec(gt, tm, rows_per_group),
                  pl.BlockSpec((1, D_MODEL), lambda i: (0, 0)),
                  _mod_spec(sc, tm, rows_per_group), _mod_spec(sh, tm, rows_per_group),
                  pl.BlockSpec((D_MODEL, D_MODEL), lambda i: (0, 0)),
                  pl.BlockSpec((D_MODEL, LANES), lambda i: (0, 0)),
                  pl.BlockSpec((1, LANES), lambda i: (0, 0))],
        out_specs=(full, full, narrow, narrow),
        compiler_params=_params(),
        name="out_proj_router",
    )(oa, ob, oc, od, x, gt.arr, g, sc.arr, sh.arr, w16, wr, br)


def _moe_kernel(h_ref, gate_ref, x1_ref, gt_ref, wg_ref, wu_ref, wd_ref, o_ref, acc_ref):
    e = pl.program_id(1)

    @pl.when(e == 0)
    def _():
        acc_ref[...] = jnp.zeros(acc_ref.shape, F32)

    h = h_ref[...]
    a = _silu(_mm(h, wg_ref[0])) * _mm(h, wu_ref[0])
    lane = lax.broadcasted_iota(jnp.int32, gate_ref.shape, 1)
    gcol = jnp.sum(jnp.where(lane == e + N_GROUPS, gate_ref[...], 0.0), axis=-1, keepdims=True)
    acc_ref[...] += gcol * _mm(a.astype(BF), wd_ref[0])

    @pl.when(e == N_EXPERTS - 1)
    def _():
        o_ref[...] = x1_ref[...] + _mod_val(gt_ref) * acc_ref[...]


def _moe(h16, gate, x1, gt, wg16, wu16, wd16, layer, rows_per_group, tm):
    n = x1.shape[0]
    full = lambda: pl.BlockSpec((tm, D_MODEL), lambda i, e: (i, 0))
    gt_spec = _mod_spec(gt, tm, rows_per_group)
    return pl.pallas_call(
        _moe_kernel,
        out_shape=jax.ShapeDtypeStruct((n, D_MODEL), F32),
        grid=(n // tm, N_EXPERTS),
        in_specs=[full(), pl.BlockSpec((tm, LANES), lambda i, e: (i, 0)), full(), gt_spec,
                  pl.BlockSpec((1, D_MODEL, D_EXPERT), lambda i, e: (layer * N_EXPERTS + e, 0, 0)),
                  pl.BlockSpec((1, D_MODEL, D_EXPERT), lambda i, e: (layer * N_EXPERTS + e, 0, 0)),
                  pl.BlockSpec((1, D_EXPERT, D_MODEL), lambda i, e: (layer * N_EXPERTS + e, 0, 0))],
        out_specs=full(),
        scratch_shapes=[pltpu.VMEM((tm, D_MODEL), F32)],
        compiler_params=_params(),
        name="moe_experts",
    )(h16, gate, x1, gt.arr, wg16, wu16, wd16)


MOE_TILE = 256
MOE_COMBINE_TILE = 256
DMA_UNROLL = 8


def _moe_plan(info, tile):
    n = info.shape[0]
    e = info[:, 0:2].astype(jnp.int32).reshape(-1)
    onehot = (e[:, None] == jnp.arange(N_EXPERTS, dtype=jnp.int32)[None, :]).astype(jnp.int32)
    csum = jnp.cumsum(onehot, axis=0)
    counts = csum[-1]
    rank = jnp.sum((csum - onehot) * onehot, axis=1)
    padded = ((counts + tile - 1) // tile) * tile
    ends = jnp.cumsum(padded)
    pos = jnp.sum(onehot * (ends - padded)[None, :], axis=1) + rank
    total = 2 * n + N_EXPERTS * tile
    src = jnp.zeros((total,), jnp.int32).at[pos].set(jnp.arange(2 * n, dtype=jnp.int32) // 2)
    tstart = jnp.arange(total // tile, dtype=jnp.int32) * tile
    texp = jnp.minimum(jnp.sum((tstart[:, None] >= ends[None, :]).astype(jnp.int32), axis=1), N_EXPERTS - 1)
    used = (ends[-1] // tile).astype(jnp.int32).reshape(1)
    return src, texp.astype(jnp.int32), used, pos.astype(jnp.int32)


def _gather_rows(idx_ref, base, count, src_hbm, dst, sem, unroll, row_of):
    def body(it, carry):
        for k in range(unroll):
            aligned, off = row_of(it, k)
            row = pl.multiple_of(aligned, 8) + off
            pltpu.make_async_copy(src_hbm.at[pl.ds(idx_ref[base + it * unroll + k], 1)],
                                  dst.at[pl.ds(row, 1)], sem).start()
        return carry
    lax.fori_loop(0, count // unroll, body, 0)


def _moe_group_kernel(src_ref, te_ref, used_ref, h_hbm, wg_ref, wu_ref, wd_ref, y_ref, xbuf, sem, *, tile):
    del te_ref
    i = pl.program_id(0)
    slot = i % 2
    used = used_ref[0]

    def start(t, s):
        _gather_rows(src_ref, t * tile, tile, h_hbm, xbuf.at[s], sem.at[s], DMA_UNROLL,
                     lambda it, k: (it * DMA_UNROLL, k))

    @pl.when(i == 0)
    def _():
        start(0, 0)

    @pl.when(i + 1 < used)
    def _():
        start(i + 1, 1 - slot)

    @pl.when(i < used)
    def _():
        pltpu.make_async_copy(xbuf.at[slot], xbuf.at[slot], sem.at[slot]).wait()
        x = xbuf[slot].astype(BF)
        a = _silu(_mm(x, wg_ref[0])) * _mm(x, wu_ref[0])
        y_ref[...] = _mm(a.astype(BF), wd_ref[0])

    @pl.when(i >= used)
    def _():
        y_ref[...] = jnp.zeros(y_ref.shape, F32)


def _moe_combine_kernel(pos_ref, y_hbm, x1_ref, gt_ref, info_ref, o_ref, ybuf, sem, *, tc, nsteps):
    i = pl.program_id(0)
    slot = i % 2

    def start(t, s):
        _gather_rows(pos_ref, t * 2 * tc, 2 * tc, y_hbm, ybuf.at[s], sem.at[s], 2 * DMA_UNROLL,
                     lambda it, k: (it * DMA_UNROLL, (k % 2) * tc + k // 2))

    @pl.when(i == 0)
    def _():
        start(0, 0)

    @pl.when(i + 1 < nsteps)
    def _():
        start(i + 1, 1 - slot)

    pltpu.make_async_copy(ybuf.at[slot], ybuf.at[slot], sem.at[slot]).wait()
    info = info_ref[...]
    y = info[:, 2:3] * ybuf[slot, 0:tc, :] + info[:, 3:4] * ybuf[slot, tc:2 * tc, :]
    o_ref[...] = x1_ref[...] + _mod_val(gt_ref) * y


def _moe_sparse(h, info, x1, gt, wg16, wu16, wd16, layer, rows_per_group):
    n = x1.shape[0]
    tile, tc = MOE_TILE, MOE_COMBINE_TILE
    src, texp, used, pos = _moe_plan(info, tile)
    total = src.shape[0]
    wspec = lambda shape: pl.BlockSpec((1,) + shape, lambda i, s, te, u: (layer * N_EXPERTS + te[i], 0, 0))
    ys = pl.pallas_call(
        functools.partial(_moe_group_kernel, tile=tile),
        out_shape=jax.ShapeDtypeStruct((total, D_MODEL), F32),
        grid_spec=pltpu.PrefetchScalarGridSpec(
            num_scalar_prefetch=3, grid=(total // tile,),
            in_specs=[pl.BlockSpec(memory_space=pl.ANY), wspec((D_MODEL, D_EXPERT)), wspec((D_MODEL, D_EXPERT)),
                      wspec((D_EXPERT, D_MODEL))],
            out_specs=pl.BlockSpec((tile, D_MODEL), lambda i, s, te, u: (i, 0)),
            scratch_shapes=[pltpu.VMEM((2, tile, D_MODEL), F32), pltpu.SemaphoreType.DMA((2,))]),
        compiler_params=_params(),
        name="moe_grouped",
    )(src, texp, used, h, wg16, wu16, wd16)
    nsteps = n // tc
    tiles = rows_per_group // tc
    return pl.pallas_call(
        functools.partial(_moe_combine_kernel, tc=tc, nsteps=nsteps),
        out_shape=jax.ShapeDtypeStruct((n, D_MODEL), F32),
        grid_spec=pltpu.PrefetchScalarGridSpec(
            num_scalar_prefetch=1, grid=(nsteps,),
            in_specs=[pl.BlockSpec(memory_space=pl.ANY),
                      pl.BlockSpec((tc, D_MODEL), lambda i, p: (i, 0)),
                      _mod_spec(gt, tc, rows_per_group),
                      pl.BlockSpec((tc, LANES), lambda i, p: (i, 0))],
            out_specs=pl.BlockSpec((tc, D_MODEL), lambda i, p: (i, 0)),
            scratch_shapes=[pltpu.VMEM((2, 2 * tc, D_MODEL), F32), pltpu.SemaphoreType.DMA((2,))]),
        compiler_params=_params(),
        name="moe_combine",
    )(pos, ys, x1, gt.arr, info)


def _final_kernel(x_ref, g_ref, o_ref):
    x = x_ref[...]
    ms = jnp.mean(x * x, axis=-1, keepdims=True)
    o_ref[...] = x * lax.rsqrt(ms + EPS) * g_ref[...]


def _final_norm(x, g, tm):
    n = x.shape[0]
    return pl.pallas_call(
        _final_kernel,
        out_shape=jax.ShapeDtypeStruct((n, D_MODEL), F32),
        grid=(n // tm,),
        in_specs=[pl.BlockSpec((tm, D_MODEL), lambda i: (i, 0)), pl.BlockSpec((1, D_MODEL), lambda i: (0, 0))],
        out_specs=pl.BlockSpec((tm, D_MODEL), lambda i: (i, 0)),
        compiler_params=_params(),
        name="final_norm",
    )(x, g)


def _state_in(s):
    b = s.shape[0]
    return s.transpose(0, 3, 1, 2).reshape(b, HEAD_DIM, GW)


def _state_out(s):
    b = s.shape[0]
    return s.reshape(b, HEAD_DIM, NH, HEAD_DIM).transpose(0, 2, 3, 1)


def _hgrn_lower_bounds(lb_logits):
    p = jax.nn.softmax(lb_logits.astype(F32), axis=0)
    return jnp.cumsum(p, axis=0) - p[0:1]


def _layer(x, mods, wl, layer, nbatch, t, pos0, s0, pbuf, past, tm):
    sh1, sc1, gt1, sh2, sc2, gt2 = mods
    n = nbatch * t
    prompt = past is None
    proj = _inproj(x, wl["g_mix"], sc1, sh1, wl["w_in"], t, tm, prompt)
    u, u16 = proj[0], proj[1]
    u3 = u.reshape(nbatch, t, IN_COLS)

    chunk = HG_CHUNK if t % HG_CHUNK == 0 else t
    groups = 2 if t % HG_CHUNK == 0 else 8
    cps = HG_CHUNKS_PER_STEP if t % (HG_CHUNK * HG_CHUNKS_PER_STEP) == 0 else 1
    oa, st = _hgrn(u3, wl["lb"], wl["hg_g"], _state_in(s0), chunk, groups, cps)

    if prompt:
        ut = proj[2]
        cos, sin = _rope_tables(np.arange(t) + pos0)
        qrot, krot_t, krot16, kmean = _rope(u, cos, sin, nbatch, MOBA_BLOCK, True)
        ob = _moba_prompt(qrot, krot16, ut, kmean.reshape(-1, GW), nbatch)
        oc = _sb_prompt(u, u16, ut, nbatch)
    else:
        cos, sin = _rope_tables(np.tile(np.arange(t), nbatch) + pos0)
        qrot, krot, _, _ = _rope(u, cos, sin, 1, n, False)
        page_table, cmk, cmv, csk, csv = past
        vb = u[:, 6 * GW:7 * GW]
        ob = _dec_attention("moba", layer, page_table, _block_diag_q(qrot, nbatch, t), krot, vb, cmk, cmv, t, pos0)
        qc, kc, vc = u[:, 7 * GW:8 * GW], u[:, 8 * GW:9 * GW], u[:, 9 * GW:10 * GW]
        oc = _dec_attention("sb", layer, page_table, _block_diag_q(qc, nbatch, t), kc, vc, csk, csv, t, pos0)

    buf_flat = jnp.pad(pbuf, ((0, 0), (POOL_HALO - POOL_BUF, 0), (0, 0))).reshape(nbatch * POOL_HALO, GW)
    od = _pool(u, buf_flat, wl["pool_w"], wl["pool_scale"], nbatch, t, pos0)
    if t >= POOL_BUF:
        buf_new = u3[:, t - POOL_BUF:, 10 * GW:]
    else:
        buf_new = jnp.concatenate([pbuf, u3[:, :, 10 * GW:]], axis=1)[:, -POOL_BUF:]

    grouped = past is None and t % MOE_COMBINE_TILE == 0
    x1, h, gate, info = _outproj(oa.reshape(n, GW), ob, oc, od, x, gt1, wl["g_ffn"], sc2, sh2,
                                 wl["w_out"], wl["wr"], wl["br"], t, tm, F32 if grouped else BF)
    if grouped:
        x2 = _moe_sparse(h, info, x1, gt2, wl["ewg"], wl["ewu"], wl["ewd"], layer, t)
    else:
        x2 = _moe(h, gate, x1, gt2, wl["ewg"], wl["ewu"], wl["ewd"], layer, t, tm)

    if prompt:
        heads = lambda zt: zt.reshape(nbatch, NH, HEAD_DIM, t).transpose(0, 3, 1, 2)
        kv = (heads(krot_t), heads(ut[:, 0:GW]), heads(ut[:, 2 * GW:3 * GW]), heads(ut[:, 3 * GW:4 * GW]))
    else:
        rows = lambda z: z.reshape(nbatch, t, NH, HEAD_DIM)
        kv = (rows(krot), rows(u[:, 6 * GW:7 * GW]), rows(u[:, 8 * GW:9 * GW]), rows(u[:, 9 * GW:10 * GW]))
    return x2, kv + (_state_out(st), buf_new)


def kernel(x_prompt, x_sample, cache_moba_k, cache_moba_v, cache_sb_k, cache_sb_v, state_hgrn, state_pool,
           page_table, c_prompt, c_sample, ada_w, ada_b, norm_mix_g, norm_ffn_g, w_in, hgrn_lb, hgrn_norm_g,
           pool_w, pool_scale, w_out, router_group_w, router_group_b, router_expert_w, router_expert_b,
           expert_w_gate, expert_w_up, expert_w_down, final_norm_g):
    bp, tp, _ = x_prompt.shape
    bs, ts, _ = x_sample.shape
    pos0_s = page_table.shape[1] * PAGE_SIZE
    n_phys = cache_moba_k.shape[1]

    c_all = jnp.concatenate([c_prompt, c_sample], axis=0)
    mpad = (-c_all.shape[0]) % 8
    mod = _ada_all(jnp.pad(c_all, ((0, mpad), (0, 0))), ada_w, ada_b)

    mod_p = mod[:, :bp].reshape(DEPTH * bp * 6, 1, D_MODEL)
    mod_s = jnp.repeat(mod[:, bp:bp + bs], ts, axis=1)

    lbs = _hgrn_lower_bounds(hgrn_lb)
    w_in16 = w_in.astype(BF)
    w_out16 = w_out.astype(BF)
    ewg16 = expert_w_gate.astype(BF).reshape(DEPTH * N_EXPERTS, D_MODEL, D_EXPERT)
    ewu16 = expert_w_up.astype(BF).reshape(DEPTH * N_EXPERTS, D_MODEL, D_EXPERT)
    ewd16 = expert_w_down.astype(BF).reshape(DEPTH * N_EXPERTS, D_EXPERT, D_MODEL)
    wr = jnp.concatenate([router_group_w, router_expert_w], axis=-1)
    wr = jnp.pad(wr, ((0, 0), (0, 0), (0, LANES - wr.shape[-1])))
    br = jnp.concatenate([router_group_b, router_expert_b], axis=-1)
    br = jnp.pad(br, ((0, 0), (0, LANES - br.shape[-1])))
    caches = tuple(c.transpose(0, 1, 3, 4, 2).reshape(DEPTH * n_phys, GW, PAGE_SIZE)
                   for c in (cache_moba_k, cache_moba_v, cache_sb_k, cache_sb_v))

    xp = x_prompt.reshape(bp * tp, D_MODEL)
    xs = x_sample.reshape(bs * ts, D_MODEL)
    st_p, st_s = [], []
    for l in range(DEPTH):
        pw_bd = jnp.zeros((GW, GW), F32)
        gc = GW // len(POOL_WINDOWS)
        for gi in range(len(POOL_WINDOWS)):
            pw_bd = lax.dynamic_update_slice(pw_bd, pool_w[l, gi], (gi * gc, gi * gc))
        wl = dict(g_mix=norm_mix_g[l][None], g_ffn=norm_ffn_g[l][None], w_in=w_in16[l], lb=lbs[l][None],
                  hg_g=hgrn_norm_g[l][None], pool_w=pw_bd.astype(BF), pool_scale=pool_scale[l][None],
                  w_out=w_out16[l], wr=wr[l], br=br[l][None], ewg=ewg16, ewu=ewu16, ewd=ewd16)
        mods_p = tuple(_Mod(mod_p, True, l, k, bp) for k in range(6))
        mods_s = tuple(_Mod(mod_s, False, l, k, bs) for k in range(6))
        xp, sp = _layer(xp, mods_p, wl, l, bp, tp, 0, jnp.zeros((bp, NH, HEAD_DIM, HEAD_DIM), F32),
                        jnp.zeros((bp, POOL_BUF, GW), F32), None, min(ROW_TILE, tp))
        xs, ss = _layer(xs, mods_s, wl, l, bs, ts, pos0_s, state_hgrn[l], state_pool[l],
                        (page_table,) + caches, min(ROW_TILE, bs * ts))
        st_p.append(sp)
        st_s.append(ss)

    y_prompt = _final_norm(xp, final_norm_g[None], min(ROW_TILE, tp)).reshape(bp, tp, D_MODEL)
    y_sample = _final_norm(xs, final_norm_g[None], min(ROW_TILE, bs * ts)).reshape(bs, ts, D_MODEL)
    stk = lambda sts, i: jnp.stack([s[i] for s in sts], axis=0)
    return (y_prompt, y_sample,
            stk(st_p, 0), stk(st_p, 1), stk(st_p, 2), stk(st_p, 3), stk(st_p, 4), stk(st_p, 5),
            stk(st_s, 0), stk(st_s, 1), stk(st_s, 2), stk(st_s, 3), stk(st_s, 4), stk(st_s, 5))
```

```python
import functools
import math
from typing import NamedTuple

import numpy as np
import jax
import jax.numpy as jnp
from jax import lax
from jax.experimental import pallas as pl
from jax.experimental.pallas import tpu as pltpu

F32 = jnp.float32
BF = jnp.bfloat16

D_MODEL = 1024
DEPTH = 4
PAGE_SIZE = 128
HEAD_DIM = 64
GW = 256
NH = 4
HG_CHUNK = 64
HG_CHUNKS_PER_STEP = 4
MOBA_BLOCK = 256
MOBA_TOPK = 3
ROT_DIM = 16
ROPE_THETA = 500000.0
POOL_WINDOWS = (2, 4, 8, 16)
POOL_BUF = 15
N_GROUPS = 4
EXP_PER_GROUP = 4
N_EXPERTS = 16
D_EXPERT = 512
IN_COLS = 11 * GW
EPS = 1e-6
NEG_BIG = -1e30
F_FLOOR = 1e-30
SCALE = 1.0 / math.sqrt(HEAD_DIM)

V7X_VMEM_BYTES = 64 * 1024 * 1024
VMEM_LIMIT = V7X_VMEM_BYTES - 12 * 1024 * 1024
LANES = 128
ROW_TILE = 512


def _params():
    return pltpu.CompilerParams(vmem_limit_bytes=VMEM_LIMIT)


def _mm(a, b):
    return jnp.dot(a, b, preferred_element_type=F32)


def _mm_nt(a, b):
    return lax.dot_general(a, b, (((1,), (1,)), ((), ())), preferred_element_type=F32)


def _mm_tn(a, b):
    return lax.dot_general(a, b, (((0,), (0,)), ((), ())), preferred_element_type=F32)


def _split2(x):
    hi = x.astype(BF)
    lo = (x - hi.astype(F32)).astype(BF)
    return hi, lo


def _split3(x):
    hi = x.astype(BF)
    r = x - hi.astype(F32)
    mid = r.astype(BF)
    lo = (r - mid.astype(F32)).astype(BF)
    return hi, mid, lo


def _mm_exact_lhs(w01, x):
    hi, mid, lo = _split3(x)
    return _mm(w01, hi) + _mm(w01, mid) + _mm(w01, lo)


def _mm3_nt(a, b):
    ah, al = _split2(a)
    bh, bl = _split2(b)
    return _mm_nt(ah, bh) + _mm_nt(ah, bl) + _mm_nt(al, bh)


def _mm3(a, b):
    ah, al = _split2(a)
    bh, bl = _split2(b)
    return _mm(ah, bh) + _mm(ah, bl) + _mm(al, bh)


def _sigmoid(x):
    return 1.0 / (1.0 + jnp.exp(-x))


def _silu(x):
    return x * _sigmoid(x)


def _log_sigmoid_pair(z):
    t = jnp.log(1.0 + jnp.exp(-jnp.abs(z)))
    return jnp.minimum(z, 0.0) - t, -jnp.maximum(z, 0.0) - t


def _ada_kernel(c_ref, w_ref, b_ref, o_ref):
    o_ref[0] = _mm(c_ref[...].astype(BF), w_ref[0].astype(BF)) + b_ref[0]


def _ada_all(c_all, ada_w, ada_b):
    m = c_all.shape[0]
    tn = 1536
    return pl.pallas_call(
        _ada_kernel,
        out_shape=jax.ShapeDtypeStruct((DEPTH, m, 6 * D_MODEL), F32),
        grid=(DEPTH, 6 * D_MODEL // tn),
        in_specs=[
            pl.BlockSpec((m, D_MODEL), lambda l, j: (0, 0)),
            pl.BlockSpec((1, D_MODEL, tn), lambda l, j: (l, 0, j)),
            pl.BlockSpec((1, 1, tn), lambda l, j: (l, 0, j)),
        ],
        out_specs=pl.BlockSpec((1, m, tn), lambda l, j: (l, 0, j)),
        compiler_params=_params(),
        name="ada_mod",
    )(c_all, ada_w, ada_b.reshape(DEPTH, 1, 6 * D_MODEL))


class _Mod(NamedTuple):
    arr: jax.Array
    per_seq: bool
    layer: int
    kind: int
    nseq: int


def _mod_spec(mod, tm, rows_per_group):
    if mod.per_seq:
        tiles = rows_per_group // tm
        base = (mod.layer * mod.nseq) * 6 + mod.kind
        return pl.BlockSpec((1, 1, D_MODEL), lambda i, *_: (base + (i // tiles) * 6, 0, 0))
    return pl.BlockSpec((1, tm, D_MODEL), lambda i, *_: (mod.layer, i, mod.kind))


def _mod_val(ref):
    return ref[0]


def _rms_mod(x, g, sc, sh):
    ms = jnp.mean(x * x, axis=-1, keepdims=True)
    return (x * lax.rsqrt(ms + EPS) * g) * (1.0 + sc) + sh


def _rope_tables(pos):
    half = ROT_DIM // 2
    inv = np.exp(-math.log(ROPE_THETA) * np.arange(half, dtype=np.float32) * 2.0 / ROT_DIM)
    ang = jnp.asarray(pos, F32)[:, None] * jnp.asarray(inv)[None, :]
    cos, sin = jnp.cos(ang), jnp.sin(ang)
    n = ang.shape[0]
    ones = jnp.ones((n, HEAD_DIM - ROT_DIM), F32)
    zeros = jnp.zeros((n, HEAD_DIM - ROT_DIM), F32)
    ch = jnp.concatenate([cos, cos, ones], axis=1)
    sh = jnp.concatenate([-sin, sin, zeros], axis=1)
    return jnp.tile(ch, (1, NH)), jnp.tile(sh, (1, NH))


def _inproj_kernel(x_ref, g_ref, sc_ref, sh_ref, w_ref, cos_ref, sin_ref,
                   u_ref, u16_ref, q_ref, k16_ref, kf_ref, *prompt_refs):
    h = _rms_mod(x_ref[...], g_ref[...], _mod_val(sc_ref), _mod_val(sh_ref))
    r = _mm(h.astype(BF), w_ref[...])
    u_ref[...] = r
    kv = r[:, 6 * GW:10 * GW]
    u16_ref[...] = kv.astype(BF)

    cos = cos_ref[...]
    sin = sin_ref[...]
    lane = lax.broadcasted_iota(jnp.int32, cos.shape, 1) % HEAD_DIM
    first = lane < ROT_DIM // 2

    def rot(x):
        partner = jnp.where(first, pltpu.roll(x, GW - ROT_DIM // 2, 1), pltpu.roll(x, ROT_DIM // 2, 1))
        return x * cos + partner * sin

    q_ref[...] = rot(r[:, 4 * GW:5 * GW])
    k = rot(r[:, 5 * GW:6 * GW])
    k16_ref[...] = k.astype(BF)
    if prompt_refs:
        ut_ref, km_ref = prompt_refs
        kf_ref[0] = jnp.transpose(k)
        ut_ref[0] = jnp.transpose(kv)
        for blk in range(k.shape[0] // MOBA_BLOCK):
            km_ref[blk] = jnp.mean(k[blk * MOBA_BLOCK:(blk + 1) * MOBA_BLOCK], axis=0, keepdims=True)
    else:
        kf_ref[...] = k


def _inproj(x, g, sc, sh, w16, cos, sin, rows_per_group, tm, prompt):
    n = x.shape[0]
    nt = rows_per_group // tm
    row = lambda w: pl.BlockSpec((tm, w), lambda i: (i, 0))
    out_shape = [jax.ShapeDtypeStruct((n, IN_COLS), F32), jax.ShapeDtypeStruct((n, 4 * GW), BF),
                 jax.ShapeDtypeStruct((n, GW), F32), jax.ShapeDtypeStruct((n, GW), BF)]
    out_specs = [row(IN_COLS), row(4 * GW), row(GW), row(GW)]
    if prompt:
        nb = n // rows_per_group
        out_shape += [jax.ShapeDtypeStruct((nb, GW, rows_per_group), F32),
                      jax.ShapeDtypeStruct((nb, 4 * GW, rows_per_group), F32),
                      jax.ShapeDtypeStruct((n // MOBA_BLOCK, 1, GW), F32)]
        out_specs += [pl.BlockSpec((1, GW, tm), lambda i: (i // nt, 0, i % nt)),
                      pl.BlockSpec((1, 4 * GW, tm), lambda i: (i // nt, 0, i % nt)),
                      pl.BlockSpec((tm // MOBA_BLOCK, 1, GW), lambda i: (i, 0, 0))]
        table = pl.BlockSpec((tm, GW), lambda i: (i % nt, 0))
    else:
        out_shape.append(jax.ShapeDtypeStruct((n, GW), F32))
        out_specs.append(row(GW))
        table = row(GW)
    return pl.pallas_call(
        _inproj_kernel,
        out_shape=tuple(out_shape),
        grid=(n // tm,),
        in_specs=[
            pl.BlockSpec((tm, D_MODEL), lambda i: (i, 0)),
            pl.BlockSpec((1, D_MODEL), lambda i: (0, 0)),
            _mod_spec(sc, tm, rows_per_group),
            _mod_spec(sh, tm, rows_per_group),
            pl.BlockSpec((D_MODEL, IN_COLS), lambda i: (0, 0)),
            table, table,
        ],
        out_specs=tuple(out_specs),
        compiler_params=_params(),
        name="in_proj",
    )(x, g, sc.arr, sh.arr, w16, cos, sin)


def _hgrn_consts(c):
    r = np.arange(c)
    mats = [(r[None, :] <= r[:, None]), (r[None, :] > r[:, None])]
    masks = []
    w = c // 2
    while w >= 1:
        blk = r // (2 * w)
        half = (r // w) % 2
        mid = blk * 2 * w + w - 1
        t_rows = (r[None, :] > mid[:, None]) & (r[None, :] <= r[:, None])
        s_rows = (r[None, :] > r[:, None]) & (r[None, :] <= mid[:, None])
        mats.append(np.where(half[:, None] == 1, t_rows, s_rows))
        masks.append((blk[:, None] == blk[None, :]) & (half[:, None] == 1) & (half[None, :] == 0))
        w //= 2
    masks.append(r[:, None] == r[None, :])
    wmat = np.concatenate(mats, axis=0).astype(np.float32)
    return jnp.asarray(wmat, BF), jnp.asarray(np.stack(masks).astype(np.float32))


def _hgrn_kernel(q_ref, f_ref, i_ref, g_ref, lb_ref, hg_ref, w_ref, mk_ref, bd_ref, s0_ref,
                 o_ref, st_ref, *, groups, chunk, cps, nlev):
    c = chunk

    @pl.when(pl.program_id(1) == 0)
    def _():
        st_ref[...] = s0_ref[...]

    lbb = jnp.clip(lb_ref[...], 0.0, 1.0)
    wmat = w_ref[...]
    for g in range(groups):
        a = f_ref[g]
        q_all = q_ref[g]
        iv16_all = i_ref[g].astype(BF)
        f = lbb + (1.0 - lbb) * _sigmoid(a)
        lf = jnp.log(jnp.maximum(f, F_FLOOR))
        kk_all = (1.0 - lbb) * _sigmoid(-a)
        lf_cat = jnp.concatenate([lf[cc * c:(cc + 1) * c] for cc in range(cps)], axis=1)
        e_cat = _mm_exact_lhs(wmat, lf_cat)
        st = st_ref[g]
        outs = []
        for cc in range(cps):
            e = e_cat[:, cc * GW:(cc + 1) * GW]
            q = q_all[cc * c:(cc + 1) * c]
            kk = kk_all[cc * c:(cc + 1) * c]
            iv16 = iv16_all[cc * c:(cc + 1) * c]
            b = e[0:c]
            qb16 = (q * jnp.exp(b)).astype(BF)
            kr16 = (kk * jnp.exp(e[c:2 * c])).astype(BF)
            dec_last = jnp.exp(b[c - 1:c])
            qt = [(q * jnp.exp(e[(2 + l) * c:(3 + l) * c])).astype(BF) for l in range(nlev)]
            kt = [(kk * jnp.exp(e[(2 + l) * c:(3 + l) * c])).astype(BF) for l in range(nlev)]
            qt.append(q.astype(BF))
            kt.append(kk.astype(BF))
            st16 = st.astype(BF)
            o_heads, st_heads = [], []
            for h in range(NH):
                sl = slice(h * HEAD_DIM, (h + 1) * HEAD_DIM)
                att = jnp.zeros((c, c), F32)
                for l in range(nlev + 1):
                    att = att + mk_ref[l] * _mm_nt(qt[l][:, sl], kt[l][:, sl])
                o_heads.append(_mm(att.astype(BF), iv16[:, sl]) + _mm_nt(qb16[:, sl], st16[:, sl]))
                st_heads.append(st[:, sl] * dec_last[:, sl] + _mm_tn(iv16[:, sl], kr16[:, sl]))
            outs.append(jnp.concatenate(o_heads, axis=1))
            st = jnp.concatenate(st_heads, axis=1)
        st_ref[g] = st
        o = jnp.concatenate(outs, axis=0)
        oh, ol = _split2(o * o)
        ms = _mm(oh, bd_ref[...]) + _mm(ol, bd_ref[...])
        o_ref[g] = o * lax.rsqrt(ms + EPS) * hg_ref[...] * _silu(g_ref[g])


def _hgrn(u3, lb, hg, s0t, chunk, groups, cps):
    bsz, t, _ = u3.shape
    nlev = int(math.log2(chunk))
    wmat, masks = _hgrn_consts(chunk)
    bd = jnp.asarray(np.kron(np.eye(NH), np.full((HEAD_DIM, HEAD_DIM), 1.0 / HEAD_DIM)), BF)
    rows = chunk * cps
    col = lambda k: pl.BlockSpec((groups, rows, GW), lambda b, c: (b, c, k))
    const = lambda shape: pl.BlockSpec(shape, lambda b, c: (0,) * len(shape))
    return pl.pallas_call(
        functools.partial(_hgrn_kernel, groups=groups, chunk=chunk, cps=cps, nlev=nlev),
        out_shape=(jax.ShapeDtypeStruct((bsz, t, GW), F32),
                   jax.ShapeDtypeStruct((bsz, HEAD_DIM, GW), F32)),
        grid=(bsz // groups, t // rows),
        in_specs=[col(0), col(1), col(2), col(3), const((1, GW)), const((1, GW)),
                  const(wmat.shape), const(masks.shape), const((GW, GW)),
                  pl.BlockSpec((groups, HEAD_DIM, GW), lambda b, c: (b, 0, 0))],
        out_specs=(pl.BlockSpec((groups, rows, GW), lambda b, c: (b, c, 0)),
                   pl.BlockSpec((groups, HEAD_DIM, GW), lambda b, c: (b, 0, 0))),
        compiler_params=_params(),
        name="hgrn2",
    )(u3, u3, u3, u3, lb, hg, wmat, masks, bd, s0t)


def _topk_rows(scores, valid, idx):
    cur = jnp.where(valid, scores, NEG_BIG)
    picked = jnp.zeros(scores.shape, F32)
    big = jnp.int32(1 << 30)
    for _ in range(MOBA_TOPK):
        mx = jnp.max(cur, axis=0, keepdims=True)
        first = jnp.min(jnp.where(cur == mx, idx, big), axis=0, keepdims=True)
        pick = idx == first
        picked = jnp.where(pick & valid, 1.0, picked)
        cur = jnp.where(pick, -jnp.inf, cur)
    return picked


def _pair_weights(qt16, tq):
    z = jnp.zeros((HEAD_DIM, tq), qt16.dtype)
    out = []
    for p in range(NH // 2):
        a = qt16[(2 * p) * HEAD_DIM:(2 * p + 1) * HEAD_DIM]
        b = qt16[(2 * p + 1) * HEAD_DIM:(2 * p + 2) * HEAD_DIM]
        out.append(jnp.concatenate([jnp.concatenate([a, z], axis=1), jnp.concatenate([z, b], axis=1)], axis=0))
    return out


def _moba_kernel(q_ref, k_ref, vt_ref, km_ref, o_ref, m_ref, l_ref, acc_ref, sel_ref, *, nb):
    own = pl.program_id(1)
    q = q_ref[...]
    tq = q.shape[0]
    qt = jnp.transpose(q)
    rowi = lax.broadcasted_iota(jnp.int32, (nb, tq), 0)
    km = km_ref[...]
    for h in range(NH):
        sl = slice(h * HEAD_DIM, (h + 1) * HEAD_DIM)
        sel_ref[h] = _topk_rows(_mm3(km[:, sl], qt[sl, :]), rowi < own, rowi)
    qp = _pair_weights((qt * SCALE).astype(BF), tq)
    m_ref[...] = jnp.full(m_ref.shape, -jnp.inf, F32)
    l_ref[...] = jnp.zeros(l_ref.shape, F32)
    acc_ref[...] = jnp.zeros(acc_ref.shape, F32)

    def step(start, nkeys, keep_fn):
        kblk = k_ref[pl.ds(start, nkeys), :]
        for p in range(NH // 2):
            sp = _mm(kblk[:, p * 2 * HEAD_DIM:(p + 1) * 2 * HEAD_DIM], qp[p])
            for r in range(2):
                h = 2 * p + r
                rows = slice(h * HEAD_DIM, (h + 1) * HEAD_DIM)
                s = jnp.where(keep_fn(h), sp[:, r * tq:(r + 1) * tq], NEG_BIG)
                m_old = m_ref[h, 0:1, :]
                m_new = jnp.maximum(m_old, jnp.max(s, axis=0, keepdims=True))
                pt = jnp.exp(s - m_new)
                alpha = jnp.exp(m_old - m_new)
                l_ref[h, 0:1, :] = alpha * l_ref[h, 0:1, :] + jnp.sum(pt, axis=0, keepdims=True)
                vt = vt_ref[0, rows, pl.ds(start, nkeys)].astype(BF)
                acc_ref[rows, :] = alpha * acc_ref[rows, :] + _mm(vt, pt.astype(BF))
                m_ref[h, 0:1, :] = m_new

    def selected(h, n):
        return jnp.broadcast_to(sel_ref[h, pl.ds(n, 1), :], (MOBA_BLOCK, tq))

    def past_pair(i, carry):
        n = 2 * i
        step(pl.multiple_of(n * MOBA_BLOCK, 2 * MOBA_BLOCK), 2 * MOBA_BLOCK,
             lambda h: jnp.concatenate([selected(h, n), selected(h, n + 1)], axis=0) > 0.5)
        return carry

    lax.fori_loop(0, own // 2, past_pair, 0)

    @pl.when(own % 2 == 1)
    def _():
        step(pl.multiple_of((own - 1) * MOBA_BLOCK, MOBA_BLOCK), MOBA_BLOCK, lambda h: selected(h, own - 1) > 0.5)

    keyi = lax.broadcasted_iota(jnp.int32, (MOBA_BLOCK, tq), 0)
    qi = lax.broadcasted_iota(jnp.int32, (MOBA_BLOCK, tq), 1)
    step(pl.multiple_of(own * MOBA_BLOCK, MOBA_BLOCK), MOBA_BLOCK, lambda h: keyi <= qi)
    out_t = jnp.concatenate(
        [acc_ref[h * HEAD_DIM:(h + 1) * HEAD_DIM, :] / l_ref[h, 0:1, :] for h in range(NH)], axis=0)
    o_ref[...] = jnp.transpose(out_t)


def _moba_prompt(qrot, k16, vt16, kmean, nbatch):
    n = qrot.shape[0]
    t = n // nbatch
    nq = t // MOBA_BLOCK
    nb = t // MOBA_BLOCK
    return pl.pallas_call(
        functools.partial(_moba_kernel, nb=nb),
        out_shape=jax.ShapeDtypeStruct((n, GW), F32),
        grid=(nbatch, nq),
        in_specs=[pl.BlockSpec((MOBA_BLOCK, GW), lambda b, j: (b * nq + j, 0)),
                  pl.BlockSpec((t, GW), lambda b, j: (b, 0)),
                  pl.BlockSpec((1, GW, t), lambda b, j: (b, 0, 0)),
                  pl.BlockSpec((nb, GW), lambda b, j: (b, 0))],
        out_specs=pl.BlockSpec((MOBA_BLOCK, GW), lambda b, j: (b * nq + j, 0)),
        scratch_shapes=[pltpu.VMEM((NH, 8, MOBA_BLOCK), F32), pltpu.VMEM((NH, 8, MOBA_BLOCK), F32),
                        pltpu.VMEM((GW, MOBA_BLOCK), F32), pltpu.VMEM((NH, nb, MOBA_BLOCK), F32)],
        compiler_params=_params(),
        name="moba_prompt",
    )(qrot, k16, vt16, kmean)


SB_BLOCK = 256
SB_KEY_STEP = 256
SB_CUTOFF = 104.0
SB_PAGES_PER_CHECK = 2


def _later_mask(n, rows_are_keys):
    r = np.arange(n)
    m = r[None, :] > r[:, None]
    return jnp.asarray((m if rows_are_keys else m.T).astype(np.float32), BF)


def _sb_kernel(q_ref, k_ref, vt_ref, up_ref, o_ref, carry_ref, acc_ref):
    j = pl.program_id(1)
    tq = q_ref.shape[0]
    qp = _pair_weights((jnp.transpose(q_ref[...]) * SCALE).astype(BF), tq)
    lower = up_ref[...]
    carry_ref[...] = jnp.zeros(carry_ref.shape, F32)
    acc_ref[...] = jnp.zeros(acc_ref.shape, F32)

    def block(start, mask):
        kblk = k_ref[pl.ds(start, SB_KEY_STEP), :]
        for p in range(NH // 2):
            zp = _mm(kblk[:, p * 2 * HEAD_DIM:(p + 1) * 2 * HEAD_DIM], qp[p])
            for r in range(2):
                h = 2 * p + r
                rows = slice(h * HEAD_DIM, (h + 1) * HEAD_DIM)
                lpos, lneg = _log_sigmoid_pair(zp[:, r * tq:(r + 1) * tq])
                if mask is not None:
                    lneg = jnp.where(mask, lneg, 0.0)
                hi, lo = _split2(lneg)
                rest = carry_ref[h:h + 1, :] + _mm(lower, hi) + _mm(lower, lo)
                a = jnp.exp(lpos + jnp.minimum(rest, 0.0))
                if mask is not None:
                    a = jnp.where(mask, a, 0.0)
                vt = vt_ref[0, rows, pl.ds(start, SB_KEY_STEP)].astype(BF)
                acc_ref[rows, :] = acc_ref[rows, :] + _mm(vt, a.astype(BF))
                carry_ref[h:h + 1, :] = carry_ref[h:h + 1, :] + jnp.sum(lneg, axis=0, keepdims=True)

    def active():
        return (jnp.max(carry_ref[0:NH, :]) > -SB_CUTOFF).astype(jnp.int32)

    keyi = lax.broadcasted_iota(jnp.int32, (SB_KEY_STEP, tq), 0)
    qi = lax.broadcasted_iota(jnp.int32, (SB_KEY_STEP, tq), 1)
    steps_per_tile = SB_BLOCK // SB_KEY_STEP
    for d in range(steps_per_tile - 1, -1, -1):
        block(pl.multiple_of(j * SB_BLOCK + d * SB_KEY_STEP, SB_KEY_STEP), keyi + d * SB_KEY_STEP < qi)

    def body(c):
        block(pl.multiple_of(c[0] * SB_KEY_STEP, SB_KEY_STEP), None)
        return c[0] - 1, active()

    lax.while_loop(lambda c: (c[0] >= 0) & (c[1] > 0), body, (j * steps_per_tile - 1, active()))
    o_ref[...] = jnp.transpose(acc_ref[...])


def _sb_prompt(u, u16, vt16, nbatch):
    n = u.shape[0]
    t = n // nbatch
    nq = t // SB_BLOCK
    return pl.pallas_call(
        _sb_kernel,
        out_shape=jax.ShapeDtypeStruct((n, GW), F32),
        grid=(nbatch, nq),
        in_specs=[pl.BlockSpec((SB_BLOCK, GW), lambda b, j: (b * nq + j, 7)),
                  pl.BlockSpec((t, GW), lambda b, j: (b, 2)),
                  pl.BlockSpec((1, GW, t), lambda b, j: (b, 3, 0)),
                  pl.BlockSpec((SB_KEY_STEP, SB_KEY_STEP), lambda b, j: (0, 0))],
        out_specs=pl.BlockSpec((SB_BLOCK, GW), lambda b, j: (b * nq + j, 0)),
        scratch_shapes=[pltpu.VMEM((8, SB_BLOCK), F32), pltpu.VMEM((GW, SB_BLOCK), F32)],
        compiler_params=_params(),
        name="sb_prompt",
    )(u, u16, vt16, _later_mask(SB_KEY_STEP, True))


def _block_diag_q(q, nbatch, t):
    q4 = q.reshape(nbatch, t, NH, HEAD_DIM)
    eye = jnp.eye(NH, dtype=q.dtype)
    bd = q4[:, None, :, :, :] * eye[None, :, None, :, None]
    return bd.reshape(nbatch, NH * t, GW)


def _page_specs(layer, n_phys, n_pages):
    def spec(p):
        return pl.BlockSpec((1, GW, PAGE_SIZE), lambda b, pt: (layer * n_phys + pt[b * n_pages + p], 0, 0))
    return [spec(p) for p in range(n_pages)]


def _heads_out(acc, den, t):
    return jnp.concatenate(
        [acc[h * t:(h + 1) * t, h * HEAD_DIM:(h + 1) * HEAD_DIM] / den[h * t:(h + 1) * t]
         for h in range(NH)], axis=1)


def _pad_rows(x, rows):
    return jnp.concatenate([x, jnp.zeros((rows - x.shape[0], x.shape[1]), x.dtype)], axis=0)


def _moba_dec_kernel(pt_ref, qbd_ref, knew_ref, vnew_ref, *refs, n_pages, t, own):
    del pt_ref
    kt_refs, vt_refs, o_ref = refs[:n_pages], refs[n_pages:2 * n_pages], refs[2 * n_pages]
    qbd = qbd_ref[0]
    nrow = qbd.shape[0]
    ppb = MOBA_BLOCK // PAGE_SIZE
    lg = [_mm3(qbd, kt_refs[p][0]) for p in range(n_pages)]
    score = []
    for n in range(own):
        tot = jnp.sum(lg[n * ppb], axis=1, keepdims=True)
        for r in range(1, ppb):
            tot = tot + jnp.sum(lg[n * ppb + r], axis=1, keepdims=True)
        score.append(tot * (1.0 / MOBA_BLOCK))
    sel = []
    for n in range(own):
        rank = jnp.zeros((nrow, 1), F32)
        for m in range(own):
            if m != n:
                beats = (score[m] >= score[n]) if m < n else (score[m] > score[n])
                rank = rank + jnp.where(beats, 1.0, 0.0)
        sel.append(rank < float(MOBA_TOPK))
    logits = [jnp.where(sel[p // ppb], lg[p] * SCALE, NEG_BIG) for p in range(n_pages)]
    lnew = _mm3_nt(qbd, _pad_rows(knew_ref[...], PAGE_SIZE)) * SCALE
    keyi = lax.broadcasted_iota(jnp.int32, lnew.shape, 1)
    qi = lax.broadcasted_iota(jnp.int32, lnew.shape, 0) % t
    lnew = jnp.where(keyi <= qi, lnew, NEG_BIG)
    m = jnp.max(lnew, axis=1, keepdims=True)
    for lp in logits:
        m = jnp.maximum(m, jnp.max(lp, axis=1, keepdims=True))
    enew = jnp.exp(lnew - m)
    den = jnp.sum(enew, axis=1, keepdims=True)
    acc = _mm(enew.astype(BF), _pad_rows(vnew_ref[...], PAGE_SIZE).astype(BF))
    for p in range(n_pages):
        e = jnp.exp(logits[p] - m)
        den = den + jnp.sum(e, axis=1, keepdims=True)
        acc = acc + _mm_nt(e.astype(BF), vt_refs[p][0].astype(BF))
    o_ref[...] = _heads_out(acc, den, t)


def _sb_dec_kernel(pt_ref, qbd_ref, knew_ref, vnew_ref, later_ref, *refs, n_pages, t):
    del pt_ref
    kt_refs, vt_refs, o_ref = refs[:n_pages], refs[n_pages:2 * n_pages], refs[2 * n_pages]
    qbd16 = (qbd_ref[0] * SCALE).astype(BF)
    nrow = qbd16.shape[0]
    later = later_ref[...]

    def block(z, carry, mask):
        lpos, lneg = _log_sigmoid_pair(z)
        if mask is not None:
            lneg = jnp.where(mask, lneg, 0.0)
        hi, lo = _split2(lneg)
        rest = carry + _mm(hi, later) + _mm(lo, later)
        a = jnp.exp(lpos + jnp.minimum(rest, 0.0))
        if mask is not None:
            a = jnp.where(mask, a, 0.0)
        return a.astype(BF), carry + jnp.sum(lneg, axis=1, keepdims=True)

    keyi = lax.broadcasted_iota(jnp.int32, (nrow, PAGE_SIZE), 1)
    qi = lax.broadcasted_iota(jnp.int32, (nrow, PAGE_SIZE), 0) % t
    znew = _mm_nt(qbd16, _pad_rows(knew_ref[...], PAGE_SIZE).astype(BF))
    a, carry = block(znew, jnp.zeros((nrow, 1), F32), keyi < qi)
    acc = _mm(a, _pad_rows(vnew_ref[...], PAGE_SIZE).astype(BF))
    for p0 in range(n_pages - 1, -1, -SB_PAGES_PER_CHECK):
        def visit(state, p0=p0):
            acc_v, carry_v = state
            for p in range(p0, max(p0 - SB_PAGES_PER_CHECK, -1), -1):
                a, carry_v = block(_mm(qbd16, kt_refs[p][0].astype(BF)), carry_v, None)
                acc_v = acc_v + _mm_nt(a, vt_refs[p][0].astype(BF))
            return acc_v, carry_v
        acc, carry = lax.cond(jnp.max(carry) > -SB_CUTOFF, visit, lambda state: state, (acc, carry))
    o_ref[...] = _heads_out(acc, jnp.ones((nrow, 1), F32), t)


def _dec_attention(kind, layer, page_table, qbd, knew, vnew, kcache, vcache, t, pos0):
    nbatch, n_pages = page_table.shape
    n_phys = kcache.shape[0] // DEPTH
    assert pos0 == n_pages * PAGE_SIZE and pos0 % MOBA_BLOCK == 0 and t <= PAGE_SIZE
    pspecs = _page_specs(layer, n_phys, n_pages)
    row = pl.BlockSpec((t, GW), lambda b, pt: (b, 0))
    in_specs = [pl.BlockSpec((1, NH * t, GW), lambda b, pt: (b, 0, 0)), row, row]
    args = [qbd, knew, vnew]
    if kind == "moba":
        body = functools.partial(_moba_dec_kernel, n_pages=n_pages, t=t, own=pos0 // MOBA_BLOCK)
    else:
        body = functools.partial(_sb_dec_kernel, n_pages=n_pages, t=t)
        in_specs.append(pl.BlockSpec((PAGE_SIZE, PAGE_SIZE), lambda b, pt: (0, 0)))
        args.append(_later_mask(PAGE_SIZE, False))
    in_specs += pspecs + pspecs
    args += [kcache] * n_pages + [vcache] * n_pages
    return pl.pallas_call(
        body,
        out_shape=jax.ShapeDtypeStruct((nbatch * t, GW), F32),
        grid_spec=pltpu.PrefetchScalarGridSpec(
            num_scalar_prefetch=1, grid=(nbatch,), in_specs=in_specs,
            out_specs=pl.BlockSpec((t, GW), lambda b, pt: (b, 0))),
        compiler_params=_params(),
        name=kind + "_decode",
    )(page_table.reshape(-1), *args)


POOL_HALO = 16


def _pool_bands(seqs, t):
    halo_seq = np.repeat(np.arange(seqs), POOL_HALO)
    halo_e = np.tile(np.arange(POOL_HALO), seqs)
    row_seq = np.repeat(np.arange(seqs), t)
    row_e = np.tile(np.arange(t), seqs) + POOL_HALO
    col_seq = np.concatenate([halo_seq, row_seq])[None, :]
    col_e = np.concatenate([halo_e, row_e])[None, :]
    same = col_seq == row_seq[:, None]
    r = row_e[:, None]
    return jnp.asarray(np.stack([(same & (col_e <= r) & (col_e > r - w)).astype(np.float32)
                                 for w in POOL_WINDOWS]), BF)


def _pool_kernel(u_ref, halo_ref, buf_ref, band_ref, pw_ref, ps_ref, o_ref, *, rows, nt, t_seq, pos0):
    i = pl.program_id(0)
    u = u_ref[...]
    halo = jnp.where(i % nt == 0, buf_ref[...], halo_ref[...])
    ext = jnp.concatenate([halo, u], axis=0)
    hi, mid, lo = _split3(ext)
    tok = (i * rows + lax.broadcasted_iota(jnp.int32, (rows, 1), 0)) % t_seq
    pos = (pos0 + tok).astype(F32)
    parts = []
    gc = GW // len(POOL_WINDOWS)
    for gi, w in enumerate(POOL_WINDOWS):
        sl = slice(gi * gc, (gi + 1) * gc)
        band = band_ref[gi]
        ssum = _mm(band, hi[:, sl]) + _mm(band, mid[:, sl]) + _mm(band, lo[:, sl])
        cnt = jnp.minimum(float(w), pos + 1.0)
        parts.append(ssum / cnt - u[:, sl])
    d = jnp.concatenate(parts, axis=1)
    o_ref[...] = _mm(d.astype(BF), pw_ref[...]) * ps_ref[...]


def _pool(u, buf_flat, pw_bd16, ps, nbatch, t, pos0):
    n = u.shape[0]
    if t >= MOBA_BLOCK:
        rows, seqs, hr = 256, 1, POOL_HALO
        nt = t // rows
        band = _pool_bands(1, rows)
        halo_arr = u
        halo_spec = pl.BlockSpec((hr, GW), lambda i: (jnp.maximum(i * (rows // hr) - 1, 0), 10))
        buf_spec = pl.BlockSpec((hr, GW), lambda i: (i // nt, 0))
    else:
        seqs = 128 // t
        rows, hr, nt = seqs * t, seqs * POOL_HALO, 1
        band = _pool_bands(seqs, t)
        halo_arr = buf_flat
        halo_spec = pl.BlockSpec((hr, GW), lambda i: (i, 0))
        buf_spec = halo_spec
    return pl.pallas_call(
        functools.partial(_pool_kernel, rows=rows, nt=nt, t_seq=t, pos0=pos0),
        out_shape=jax.ShapeDtypeStruct((n, GW), F32),
        grid=(n // rows,),
        in_specs=[pl.BlockSpec((rows, GW), lambda i: (i, 10)), halo_spec, buf_spec,
                  pl.BlockSpec(band.shape, lambda i: (0, 0, 0)),
                  pl.BlockSpec((GW, GW), lambda i: (0, 0)),
                  pl.BlockSpec((1, GW), lambda i: (0, 0))],
        out_specs=pl.BlockSpec((rows, GW), lambda i: (i, 0)),
        compiler_params=_params(),
        name="pool_mixer",
    )(u, halo_arr, buf_flat, band, pw_bd16, ps)


def _route(logits):
    lane = lax.broadcasted_iota(jnp.int32, logits.shape, 1)
    big = jnp.int32(1 << 30)
    gmask = lane < N_GROUPS
    gl = jnp.where(gmask, logits, -jnp.inf)
    gmax = jnp.max(gl, axis=-1, keepdims=True)
    gsum = jnp.sum(jnp.where(gmask, jnp.exp(gl - gmax), 0.0), axis=-1, keepdims=True)
    g_top = 1.0 / gsum
    g_idx = jnp.min(jnp.where(gl == gmax, lane, big), axis=-1, keepdims=True)
    emask = (lane >= N_GROUPS) & (lane < N_GROUPS + N_EXPERTS) & ((lane - N_GROUPS) // EXP_PER_GROUP == g_idx)
    el = jnp.where(emask, logits, -jnp.inf)
    m1 = jnp.max(el, axis=-1, keepdims=True)
    i1 = jnp.min(jnp.where(el == m1, lane, big), axis=-1, keepdims=True)
    el2 = jnp.where(lane == i1, -jnp.inf, el)
    m2 = jnp.max(el2, axis=-1, keepdims=True)
    i2 = jnp.min(jnp.where(el2 == m2, lane, big), axis=-1, keepdims=True)
    r = jnp.exp(m2 - m1)
    w1 = g_top / (1.0 + r)
    w2 = g_top * r / (1.0 + r)
    gate = jnp.where(lane == i1, w1, jnp.where(lane == i2, w2, 0.0))
    info = jnp.where(lane == 0, (i1 - N_GROUPS).astype(F32),
                     jnp.where(lane == 1, (i2 - N_GROUPS).astype(F32),
                               jnp.where(lane == 2, w1, jnp.where(lane == 3, w2, 0.0))))
    return gate, info


def _outproj_kernel(oa_ref, ob_ref, oc_ref, od_ref, x_ref, gt_ref, g_ref, sc_ref, sh_ref, w_ref,
                    wr_ref, br_ref, x1_ref, h_ref, gate_ref, info_ref):
    merged = jnp.concatenate([oa_ref[...], ob_ref[...], oc_ref[...], od_ref[...]], axis=1).astype(BF)
    x1 = x_ref[...] + _mod_val(gt_ref) * _mm(merged, w_ref[...])
    x1_ref[...] = x1
    h = _rms_mod(x1, g_ref[...], _mod_val(sc_ref), _mod_val(sh_ref))
    h_ref[...] = h.astype(h_ref.dtype)
    gate_ref[...], info_ref[...] = _route(_mm3(h, wr_ref[...]) + br_ref[...])


def _outproj(oa, ob, oc, od, x, gt, g, sc, sh, w16, wr, br, rows_per_group, tm, h_dtype):
    n = x.shape[0]
    quarter = pl.BlockSpec((tm, GW), lambda i: (i, 0))
    full = pl.BlockSpec((tm, D_MODEL), lambda i: (i, 0))
    narrow = pl.BlockSpec((tm, LANES), lambda i: (i, 0))
    return pl.pallas_call(
        _outproj_kernel,
        out_shape=(jax.ShapeDtypeStruct((n, D_MODEL), F32), jax.ShapeDtypeStruct((n, D_MODEL), h_dtype),
                   jax.ShapeDtypeStruct((n, LANES), F32), jax.ShapeDtypeStruct((n, LANES), F32)),
        grid=(n // tm,),
        in_specs=[quarter, quarter, quarter, quarter, full,
                  _mod_spec(gt, tm, rows_per_group),
                  pl.BlockSpec((1, D_MODEL), lambda i: (0, 0)),
                  _mod_spec(sc, tm, rows_per_group), _mod_spec(sh, tm, rows_per_group),
                  pl.BlockSpec((D_MODEL, D_MODEL), lambda i: (0, 0)),
                  pl.BlockSpec((D_MODEL, LANES), lambda i: (0, 0)),
                  pl.BlockSpec((1, LANES), lambda i: (0, 0))],
        out_specs=(full, full, narrow, narrow),
        compiler_params=_params(),
        name="out_proj_router",
    )(oa, ob, oc, od, x, gt.arr, g, sc.arr, sh.arr, w16, wr, br)


def _moe_kernel(h_ref, gate_ref, x1_ref, gt_ref, wg_ref, wu_ref, wd_ref, o_ref, acc_ref):
    e = pl.program_id(1)

    @pl.when(e == 0)
    def _():
        acc_ref[...] = jnp.zeros(acc_ref.shape, F32)

    h = h_ref[...]
    a = _silu(_mm(h, wg_ref[0].astype(BF))) * _mm(h, wu_ref[0].astype(BF))
    lane = lax.broadcasted_iota(jnp.int32, gate_ref.shape, 1)
    gcol = jnp.sum(jnp.where(lane == e + N_GROUPS, gate_ref[...], 0.0), axis=-1, keepdims=True)
    acc_ref[...] += gcol * _mm(a.astype(BF), wd_ref[0].astype(BF))

    @pl.when(e == N_EXPERTS - 1)
    def _():
        o_ref[...] = x1_ref[...] + _mod_val(gt_ref) * acc_ref[...]


def _moe(h16, gate, x1, gt, wg16, wu16, wd16, layer, rows_per_group, tm):
    n = x1.shape[0]
    full = lambda: pl.BlockSpec((tm, D_MODEL), lambda i, e: (i, 0))
    gt_spec = _mod_spec(gt, tm, rows_per_group)
    return pl.pallas_call(
        _moe_kernel,
        out_shape=jax.ShapeDtypeStruct((n, D_MODEL), F32),
        grid=(n // tm, N_EXPERTS),
        in_specs=[full(), pl.BlockSpec((tm, LANES), lambda i, e: (i, 0)), full(), gt_spec,
                  pl.BlockSpec((1, D_MODEL, D_EXPERT), lambda i, e: (layer * N_EXPERTS + e, 0, 0)),
                  pl.BlockSpec((1, D_MODEL, D_EXPERT), lambda i, e: (layer * N_EXPERTS + e, 0, 0)),
                  pl.BlockSpec((1, D_EXPERT, D_MODEL), lambda i, e: (layer * N_EXPERTS + e, 0, 0))],
        out_specs=full(),
        scratch_shapes=[pltpu.VMEM((tm, D_MODEL), F32)],
        compiler_params=_params(),
        name="moe_experts",
    )(h16, gate, x1, gt.arr, wg16, wu16, wd16)


MOE_TILE = 256
MOE_COMBINE_TILE = 256
DMA_UNROLL = 8


def _moe_plan(info, tile):
    n = info.shape[0]
    e = info[:, 0:2].astype(jnp.int32).reshape(-1)
    onehot = (e[:, None] == jnp.arange(N_EXPERTS, dtype=jnp.int32)[None, :]).astype(jnp.int32)
    csum = jnp.cumsum(onehot, axis=0)
    counts = csum[-1]
    rank = jnp.sum((csum - onehot) * onehot, axis=1)
    padded = ((counts + tile - 1) // tile) * tile
    ends = jnp.cumsum(padded)
    pos = jnp.sum(onehot * (ends - padded)[None, :], axis=1) + rank
    total = 2 * n + N_EXPERTS * tile
    src = jnp.zeros((total,), jnp.int32).at[pos].set(jnp.arange(2 * n, dtype=jnp.int32) // 2)
    tstart = jnp.arange(total // tile, dtype=jnp.int32) * tile
    texp = jnp.minimum(jnp.sum((tstart[:, None] >= ends[None, :]).astype(jnp.int32), axis=1), N_EXPERTS - 1)
    used = (ends[-1] // tile).astype(jnp.int32).reshape(1)
    return src, texp.astype(jnp.int32), used, pos.astype(jnp.int32)


def _gather_rows(idx_ref, base, count, src_hbm, dst, sem, unroll, row_of):
    def body(it, carry):
        for k in range(unroll):
            aligned, off = row_of(it, k)
            row = pl.multiple_of(aligned, 8) + off
            pltpu.make_async_copy(src_hbm.at[pl.ds(idx_ref[base + it * unroll + k], 1)],
                                  dst.at[pl.ds(row, 1)], sem).start()
        return carry
    lax.fori_loop(0, count // unroll, body, 0)


def _moe_group_kernel(src_ref, te_ref, used_ref, h_hbm, wg_ref, wu_ref, wd_ref, y_ref, xbuf, sem, *, tile):
    del te_ref
    i = pl.program_id(0)
    slot = i % 2
    used = used_ref[0]

    def start(t, s):
        _gather_rows(src_ref, t * tile, tile, h_hbm, xbuf.at[s], sem.at[s], DMA_UNROLL,
                     lambda it, k: (it * DMA_UNROLL, k))

    @pl.when(i == 0)
    def _():
        start(0, 0)

    @pl.when(i + 1 < used)
    def _():
        start(i + 1, 1 - slot)

    @pl.when(i < used)
    def _():
        pltpu.make_async_copy(xbuf.at[slot], xbuf.at[slot], sem.at[slot]).wait()
        x = xbuf[slot].astype(BF)
        a = _silu(_mm(x, wg_ref[0].astype(BF))) * _mm(x, wu_ref[0].astype(BF))
        y_ref[...] = _mm(a.astype(BF), wd_ref[0].astype(BF))

    @pl.when(i >= used)
    def _():
        y_ref[...] = jnp.zeros(y_ref.shape, F32)


def _moe_combine_kernel(pos_ref, y_hbm, x1_ref, gt_ref, info_ref, o_ref, ybuf, sem, *, tc, nsteps):
    i = pl.program_id(0)
    slot = i % 2

    def start(t, s):
        _gather_rows(pos_ref, t * 2 * tc, 2 * tc, y_hbm, ybuf.at[s], sem.at[s], 2 * DMA_UNROLL,
                     lambda it, k: (it * DMA_UNROLL, (k % 2) * tc + k // 2))

    @pl.when(i == 0)
    def _():
        start(0, 0)

    @pl.when(i + 1 < nsteps)
    def _():
        start(i + 1, 1 - slot)

    pltpu.make_async_copy(ybuf.at[slot], ybuf.at[slot], sem.at[slot]).wait()
    info = info_ref[...]
    y = info[:, 2:3] * ybuf[slot, 0:tc, :] + info[:, 3:4] * ybuf[slot, tc:2 * tc, :]
    o_ref[...] = x1_ref[...] + _mod_val(gt_ref) * y


def _moe_sparse(h, info, x1, gt, wg16, wu16, wd16, layer, rows_per_group):
    n = x1.shape[0]
    tile, tc = MOE_TILE, MOE_COMBINE_TILE
    src, texp, used, pos = _moe_plan(info, tile)
    total = src.shape[0]
    wspec = lambda shape: pl.BlockSpec((1,) + shape, lambda i, s, te, u: (layer * N_EXPERTS + te[i], 0, 0))
    ys = pl.pallas_call(
        functools.partial(_moe_group_kernel, tile=tile),
        out_shape=jax.ShapeDtypeStruct((total, D_MODEL), F32),
        grid_spec=pltpu.PrefetchScalarGridSpec(
            num_scalar_prefetch=3, grid=(total // tile,),
            in_specs=[pl.BlockSpec(memory_space=pl.ANY), wspec((D_MODEL, D_EXPERT)), wspec((D_MODEL, D_EXPERT)),
                      wspec((D_EXPERT, D_MODEL))],
            out_specs=pl.BlockSpec((tile, D_MODEL), lambda i, s, te, u: (i, 0)),
            scratch_shapes=[pltpu.VMEM((2, tile, D_MODEL), F32), pltpu.SemaphoreType.DMA((2,))]),
        compiler_params=_params(),
        name="moe_grouped",
    )(src, texp, used, h, wg16, wu16, wd16)
    nsteps = n // tc
    tiles = rows_per_group // tc
    return pl.pallas_call(
        functools.partial(_moe_combine_kernel, tc=tc, nsteps=nsteps),
        out_shape=jax.ShapeDtypeStruct((n, D_MODEL), F32),
        grid_spec=pltpu.PrefetchScalarGridSpec(
            num_scalar_prefetch=1, grid=(nsteps,),
            in_specs=[pl.BlockSpec(memory_space=pl.ANY),
                      pl.BlockSpec((tc, D_MODEL), lambda i, p: (i, 0)),
                      _mod_spec(gt, tc, rows_per_group),
                      pl.BlockSpec((tc, LANES), lambda i, p: (i, 0))],
            out_specs=pl.BlockSpec((tc, D_MODEL), lambda i, p: (i, 0)),
            scratch_shapes=[pltpu.VMEM((2, 2 * tc, D_MODEL), F32), pltpu.SemaphoreType.DMA((2,))]),
        compiler_params=_params(),
        name="moe_combine",
    )(pos, ys, x1, gt.arr, info)


def _final_kernel(x_ref, g_ref, o_ref):
    x = x_ref[...]
    ms = jnp.mean(x * x, axis=-1, keepdims=True)
    o_ref[...] = x * lax.rsqrt(ms + EPS) * g_ref[...]


def _final_norm(x, g, tm):
    n = x.shape[0]
    return pl.pallas_call(
        _final_kernel,
        out_shape=jax.ShapeDtypeStruct((n, D_MODEL), F32),
        grid=(n // tm,),
        in_specs=[pl.BlockSpec((tm, D_MODEL), lambda i: (i, 0)), pl.BlockSpec((1, D_MODEL), lambda i: (0, 0))],
        out_specs=pl.BlockSpec((tm, D_MODEL), lambda i: (i, 0)),
        compiler_params=_params(),
        name="final_norm",
    )(x, g)


def _state_in(s):
    b = s.shape[0]
    return s.transpose(0, 3, 1, 2).reshape(b, HEAD_DIM, GW)


def _state_out(s):
    b = s.shape[0]
    return s.reshape(b, HEAD_DIM, NH, HEAD_DIM).transpose(0, 2, 3, 1)


def _hgrn_lower_bounds(lb_logits):
    p = jax.nn.softmax(lb_logits.astype(F32), axis=0)
    return jnp.cumsum(p, axis=0) - p[0:1]


def _layer(x, mods, wl, layer, nbatch, t, pos0, s0, pbuf, past, tm):
    sh1, sc1, gt1, sh2, sc2, gt2 = mods
    n = nbatch * t
    prompt = past is None
    cos, sin = _rope_tables((np.arange(t) if prompt else np.tile(np.arange(t), nbatch)) + pos0)
    proj = _inproj(x, wl["g_mix"], sc1, sh1, wl["w_in"], cos, sin, t, tm, prompt)
    u, u16, qrot, krot16, krot = proj[:5]
    u3 = u.reshape(nbatch, t, IN_COLS)

    chunk = HG_CHUNK if t % HG_CHUNK == 0 else t
    groups = 2 if t % HG_CHUNK == 0 else 8
    cps = HG_CHUNKS_PER_STEP if t % (HG_CHUNK * HG_CHUNKS_PER_STEP) == 0 else 1
    oa, st = _hgrn(u3, wl["lb"], wl["hg_g"], _state_in(s0), chunk, groups, cps)

    if prompt:
        krot_t, ut, kmean = krot, proj[5], proj[6]
        ob = _moba_prompt(qrot, krot16, ut, kmean.reshape(-1, GW), nbatch)
        oc = _sb_prompt(u, u16, ut, nbatch)
    else:
        page_table, cmk, cmv, csk, csv = past
        vb = u[:, 6 * GW:7 * GW]
        ob = _dec_attention("moba", layer, page_table, _block_diag_q(qrot, nbatch, t), krot, vb, cmk, cmv, t, pos0)
        qc, kc, vc = u[:, 7 * GW:8 * GW], u[:, 8 * GW:9 * GW], u[:, 9 * GW:10 * GW]
        oc = _dec_attention("sb", layer, page_table, _block_diag_q(qc, nbatch, t), kc, vc, csk, csv, t, pos0)

    buf_flat = jnp.pad(pbuf, ((0, 0), (POOL_HALO - POOL_BUF, 0), (0, 0))).reshape(nbatch * POOL_HALO, GW)
    od = _pool(u, buf_flat, wl["pool_w"], wl["pool_scale"], nbatch, t, pos0)
    if t >= POOL_BUF:
        buf_new = u3[:, t - POOL_BUF:, 10 * GW:]
    else:
        buf_new = jnp.concatenate([pbuf, u3[:, :, 10 * GW:]], axis=1)[:, -POOL_BUF:]

    grouped = past is None and t % MOE_COMBINE_TILE == 0
    x1, h, gate, info = _outproj(oa.reshape(n, GW), ob, oc, od, x, gt1, wl["g_ffn"], sc2, sh2,
                                 wl["w_out"], wl["wr"], wl["br"], t, tm, F32 if grouped else BF)
    if grouped:
        x2 = _moe_sparse(h, info, x1, gt2, wl["ewg"], wl["ewu"], wl["ewd"], layer, t)
    else:
        x2 = _moe(h, gate, x1, gt2, wl["ewg"], wl["ewu"], wl["ewd"], layer, t, tm)

    if prompt:
        heads = lambda zt: zt.reshape(nbatch, NH, HEAD_DIM, t).transpose(0, 3, 1, 2)
        kv = (heads(krot_t), heads(ut[:, 0:GW]), heads(ut[:, 2 * GW:3 * GW]), heads(ut[:, 3 * GW:4 * GW]))
    else:
        rows = lambda z: z.reshape(nbatch, t, NH, HEAD_DIM)
        kv = (rows(krot), rows(u[:, 6 * GW:7 * GW]), rows(u[:, 8 * GW:9 * GW]), rows(u[:, 9 * GW:10 * GW]))
    return x2, kv + (_state_out(st), buf_new)


def kernel(x_prompt, x_sample, cache_moba_k, cache_moba_v, cache_sb_k, cache_sb_v, state_hgrn, state_pool,
           page_table, c_prompt, c_sample, ada_w, ada_b, norm_mix_g, norm_ffn_g, w_in, hgrn_lb, hgrn_norm_g,
           pool_w, pool_scale, w_out, router_group_w, router_group_b, router_expert_w, router_expert_b,
           expert_w_gate, expert_w_up, expert_w_down, final_norm_g):
    bp, tp, _ = x_prompt.shape
    bs, ts, _ = x_sample.shape
    pos0_s = page_table.shape[1] * PAGE_SIZE
    n_phys = cache_moba_k.shape[1]

    c_all = jnp.concatenate([c_prompt, c_sample], axis=0)
    mpad = (-c_all.shape[0]) % 8
    mod = _ada_all(jnp.pad(c_all, ((0, mpad), (0, 0))), ada_w, ada_b)

    mod_p = mod[:, :bp].reshape(DEPTH * bp * 6, 1, D_MODEL)
    mod_s = jnp.repeat(mod[:, bp:bp + bs], ts, axis=1)

    lbs = _hgrn_lower_bounds(hgrn_lb)
    w_in16 = w_in.astype(BF)
    w_out16 = w_out.astype(BF)
    ewg_all = expert_w_gate.reshape(DEPTH * N_EXPERTS, D_MODEL, D_EXPERT)
    ewu_all = expert_w_up.reshape(DEPTH * N_EXPERTS, D_MODEL, D_EXPERT)
    ewd_all = expert_w_down.reshape(DEPTH * N_EXPERTS, D_EXPERT, D_MODEL)
    wr = jnp.concatenate([router_group_w, router_expert_w], axis=-1)
    wr = jnp.pad(wr, ((0, 0), (0, 0), (0, LANES - wr.shape[-1])))
    br = jnp.concatenate([router_group_b, router_expert_b], axis=-1)
    br = jnp.pad(br, ((0, 0), (0, LANES - br.shape[-1])))
    caches = tuple(c.transpose(0, 1, 3, 4, 2).reshape(DEPTH * n_phys, GW, PAGE_SIZE)
                   for c in (cache_moba_k, cache_moba_v, cache_sb_k, cache_sb_v))

    xp = x_prompt.reshape(bp * tp, D_MODEL)
    xs = x_sample.reshape(bs * ts, D_MODEL)
    st_p, st_s = [], []
    for l in range(DEPTH):
        pw_bd = jnp.zeros((GW, GW), F32)
        gc = GW // len(POOL_WINDOWS)
        for gi in range(len(POOL_WINDOWS)):
            pw_bd = lax.dynamic_update_slice(pw_bd, pool_w[l, gi], (gi * gc, gi * gc))
        wl = dict(g_mix=norm_mix_g[l][None], g_ffn=norm_ffn_g[l][None], w_in=w_in16[l], lb=lbs[l][None],
                  hg_g=hgrn_norm_g[l][None], pool_w=pw_bd.astype(BF), pool_scale=pool_scale[l][None],
                  w_out=w_out16[l], wr=wr[l], br=br[l][None], ewg=ewg_all, ewu=ewu_all, ewd=ewd_all)
        mods_p = tuple(_Mod(mod_p, True, l, k, bp) for k in range(6))
        mods_s = tuple(_Mod(mod_s, False, l, k, bs) for k in range(6))
        xp, sp = _layer(xp, mods_p, wl, l, bp, tp, 0, jnp.zeros((bp, NH, HEAD_DIM, HEAD_DIM), F32),
                        jnp.zeros((bp, POOL_BUF, GW), F32), None, min(ROW_TILE, tp))
        xs, ss = _layer(xs, mods_s, wl, l, bs, ts, pos0_s, state_hgrn[l], state_pool[l],
                        (page_table,) + caches, min(ROW_TILE, bs * ts))
        st_p.append(sp)
        st_s.append(ss)

    y_prompt = _final_norm(xp, final_norm_g[None], min(ROW_TILE, tp)).reshape(bp, tp, D_MODEL)
    y_sample = _final_norm(xs, final_norm_g[None], min(ROW_TILE, bs * ts)).reshape(bs, ts, D_MODEL)
    stk = lambda sts, i: jnp.stack([s[i] for s in sts], axis=0)
    return (y_prompt, y_sample,
            stk(st_p, 0), stk(st_p, 1), stk(st_p, 2), stk(st_p, 3), stk(st_p, 4), stk(st_p, 5),
            stk(st_s, 0), stk(st_s, 1), stk(st_s, 2), stk(st_s, 3), stk(st_s, 4), stk(st_s, 5))
```

```python
import functools
import math
from typing import NamedTuple

import numpy as np
import jax
import jax.numpy as jnp
from jax import lax
from jax.experimental import pallas as pl
from jax.experimental.pallas import tpu as pltpu

F32 = jnp.float32
BF = jnp.bfloat16

D_MODEL = 1024
DEPTH = 4
PAGE_SIZE = 128
HEAD_DIM = 64
GW = 256
NH = 4
HG_CHUNK = 64
HG_CHUNKS_PER_STEP = 4
MOBA_BLOCK = 256
MOBA_TOPK = 3
ROT_DIM = 16
ROPE_THETA = 500000.0
POOL_WINDOWS = (2, 4, 8, 16)
POOL_BUF = 15
N_GROUPS = 4
EXP_PER_GROUP = 4
N_EXPERTS = 16
D_EXPERT = 512
IN_COLS = 11 * GW
EPS = 1e-6
NEG_BIG = -1e30
F_FLOOR = 1e-30
SCALE = 1.0 / math.sqrt(HEAD_DIM)

V7X_VMEM_BYTES = 64 * 1024 * 1024
VMEM_LIMIT = V7X_VMEM_BYTES - 12 * 1024 * 1024
LANES = 128
ROW_TILE = 512


def _params():
    return pltpu.CompilerParams(vmem_limit_bytes=VMEM_LIMIT)


def _mm(a, b):
    return jnp.dot(a, b, preferred_element_type=F32)


def _mm_nt(a, b):
    return lax.dot_general(a, b, (((1,), (1,)), ((), ())), preferred_element_type=F32)


def _mm_tn(a, b):
    return lax.dot_general(a, b, (((0,), (0,)), ((), ())), preferred_element_type=F32)


def _split2(x):
    hi = x.astype(BF)
    lo = (x - hi.astype(F32)).astype(BF)
    return hi, lo


def _split3(x):
    hi = x.astype(BF)
    r = x - hi.astype(F32)
    mid = r.astype(BF)
    lo = (r - mid.astype(F32)).astype(BF)
    return hi, mid, lo


def _mm_exact_lhs(w01, x):
    hi, mid, lo = _split3(x)
    return _mm(w01, hi) + _mm(w01, mid) + _mm(w01, lo)


def _mm3_nt(a, b):
    ah, al = _split2(a)
    bh, bl = _split2(b)
    return _mm_nt(ah, bh) + _mm_nt(ah, bl) + _mm_nt(al, bh)


def _mm3(a, b):
    ah, al = _split2(a)
    bh, bl = _split2(b)
    return _mm(ah, bh) + _mm(ah, bl) + _mm(al, bh)


def _sigmoid(x):
    return 1.0 / (1.0 + jnp.exp(-x))


def _silu(x):
    return x * _sigmoid(x)


def _log_sigmoid_pair(z):
    t = jnp.log(1.0 + jnp.exp(-jnp.abs(z)))
    return jnp.minimum(z, 0.0) - t, -jnp.maximum(z, 0.0) - t


def _ada_kernel(c_ref, w_ref, b_ref, o_ref):
    o_ref[0] = _mm(c_ref[...].astype(BF), w_ref[0].astype(BF)) + b_ref[0]


def _ada_all(c_all, ada_w, ada_b):
    m = c_all.shape[0]
    tn = 1536
    return pl.pallas_call(
        _ada_kernel,
        out_shape=jax.ShapeDtypeStruct((DEPTH, m, 6 * D_MODEL), F32),
        grid=(DEPTH, 6 * D_MODEL // tn),
        in_specs=[
            pl.BlockSpec((m, D_MODEL), lambda l, j: (0, 0)),
            pl.BlockSpec((1, D_MODEL, tn), lambda l, j: (l, 0, j)),
            pl.BlockSpec((1, 1, tn), lambda l, j: (l, 0, j)),
        ],
        out_specs=pl.BlockSpec((1, m, tn), lambda l, j: (l, 0, j)),
        compiler_params=_params(),
        name="ada_mod",
    )(c_all, ada_w, ada_b.reshape(DEPTH, 1, 6 * D_MODEL))


class _Mod(NamedTuple):
    arr: jax.Array
    per_seq: bool
    layer: int
    kind: int
    nseq: int


def _mod_spec(mod, tm, rows_per_group):
    if mod.per_seq:
        tiles = rows_per_group // tm
        base = (mod.layer * mod.nseq) * 6 + mod.kind
        return pl.BlockSpec((1, 1, D_MODEL), lambda i, *_: (base + (i // tiles) * 6, 0, 0))
    return pl.BlockSpec((1, tm, D_MODEL), lambda i, *_: (mod.layer, i, mod.kind))


def _mod_val(ref):
    return ref[0]


def _rms_mod(x, g, sc, sh):
    ms = jnp.mean(x * x, axis=-1, keepdims=True)
    return (x * lax.rsqrt(ms + EPS) * g) * (1.0 + sc) + sh


def _rope_tables(pos):
    half = ROT_DIM // 2
    inv = jnp.exp(-math.log(ROPE_THETA) * jnp.arange(half, dtype=F32) * 2.0 / ROT_DIM)
    ang = jnp.asarray(pos).astype(F32)[:, None] * inv[None, :]
    cos, sin = jnp.cos(ang), jnp.sin(ang)
    n = ang.shape[0]
    ones = jnp.ones((n, HEAD_DIM - ROT_DIM), F32)
    zeros = jnp.zeros((n, HEAD_DIM - ROT_DIM), F32)
    ch = jnp.concatenate([cos, cos, ones], axis=1)
    sh = jnp.concatenate([-sin, sin, zeros], axis=1)
    return jnp.tile(ch, (1, NH)), jnp.tile(sh, (1, NH))


def _inproj_kernel(x_ref, g_ref, sc_ref, sh_ref, w_ref, cos_ref, sin_ref,
                   u_ref, u16_ref, q_ref, k16_ref, kf_ref, *prompt_refs):
    h = _rms_mod(x_ref[...], g_ref[...], _mod_val(sc_ref), _mod_val(sh_ref))
    r = _mm(h.astype(BF), w_ref[...])
    u_ref[...] = r
    kv = r[:, 6 * GW:10 * GW]
    u16_ref[...] = kv.astype(BF)

    cos = cos_ref[...]
    sin = sin_ref[...]
    lane = lax.broadcasted_iota(jnp.int32, cos.shape, 1) % HEAD_DIM
    first = lane < ROT_DIM // 2

    def rot(x):
        partner = jnp.where(first, pltpu.roll(x, GW - ROT_DIM // 2, 1), pltpu.roll(x, ROT_DIM // 2, 1))
        return x * cos + partner * sin

    q_ref[...] = rot(r[:, 4 * GW:5 * GW])
    k = rot(r[:, 5 * GW:6 * GW])
    k16_ref[...] = k.astype(BF)
    if prompt_refs:
        ut_ref, km_ref = prompt_refs
        kf_ref[0] = jnp.transpose(k)
        ut_ref[0] = jnp.transpose(kv)
        for blk in range(k.shape[0] // MOBA_BLOCK):
            km_ref[blk] = jnp.mean(k[blk * MOBA_BLOCK:(blk + 1) * MOBA_BLOCK], axis=0, keepdims=True)
    else:
        kf_ref[...] = k


def _inproj(x, g, sc, sh, w16, cos, sin, rows_per_group, tm, prompt):
    n = x.shape[0]
    nt = rows_per_group // tm
    row = lambda w: pl.BlockSpec((tm, w), lambda i: (i, 0))
    out_shape = [jax.ShapeDtypeStruct((n, IN_COLS), F32), jax.ShapeDtypeStruct((n, 4 * GW), BF),
                 jax.ShapeDtypeStruct((n, GW), F32), jax.ShapeDtypeStruct((n, GW), BF)]
    out_specs = [row(IN_COLS), row(4 * GW), row(GW), row(GW)]
    if prompt:
        nb = n // rows_per_group
        out_shape += [jax.ShapeDtypeStruct((nb, GW, rows_per_group), F32),
                      jax.ShapeDtypeStruct((nb, 4 * GW, rows_per_group), F32),
                      jax.ShapeDtypeStruct((n // MOBA_BLOCK, 1, GW), F32)]
        out_specs += [pl.BlockSpec((1, GW, tm), lambda i: (i // nt, 0, i % nt)),
                      pl.BlockSpec((1, 4 * GW, tm), lambda i: (i // nt, 0, i % nt)),
                      pl.BlockSpec((tm // MOBA_BLOCK, 1, GW), lambda i: (i, 0, 0))]
        table = pl.BlockSpec((tm, GW), lambda i: (i % nt, 0))
    else:
        out_shape.append(jax.ShapeDtypeStruct((n, GW), F32))
        out_specs.append(row(GW))
        table = row(GW)
    return pl.pallas_call(
        _inproj_kernel,
        out_shape=tuple(out_shape),
        grid=(n // tm,),
        in_specs=[
            pl.BlockSpec((tm, D_MODEL), lambda i: (i, 0)),
            pl.BlockSpec((1, D_MODEL), lambda i: (0, 0)),
            _mod_spec(sc, tm, rows_per_group),
            _mod_spec(sh, tm, rows_per_group),
            pl.BlockSpec((D_MODEL, IN_COLS), lambda i: (0, 0)),
            table, table,
        ],
        out_specs=tuple(out_specs),
        compiler_params=_params(),
        name="in_proj",
    )(x, g, sc.arr, sh.arr, w16, cos, sin)


def _hgrn_consts(c):
    r = np.arange(c)
    mats = [(r[None, :] <= r[:, None]), (r[None, :] > r[:, None])]
    masks = []
    w = c // 2
    while w >= 1:
        blk = r // (2 * w)
        half = (r // w) % 2
        mid = blk * 2 * w + w - 1
        t_rows = (r[None, :] > mid[:, None]) & (r[None, :] <= r[:, None])
        s_rows = (r[None, :] > r[:, None]) & (r[None, :] <= mid[:, None])
        mats.append(np.where(half[:, None] == 1, t_rows, s_rows))
        masks.append((blk[:, None] == blk[None, :]) & (half[:, None] == 1) & (half[None, :] == 0))
        w //= 2
    masks.append(r[:, None] == r[None, :])
    wmat = np.concatenate(mats, axis=0).astype(np.float32)
    return jnp.asarray(wmat, BF), jnp.asarray(np.stack(masks).astype(np.float32))


def _hgrn_kernel(q_ref, f_ref, i_ref, g_ref, lb_ref, hg_ref, w_ref, mk_ref, bd_ref, s0_ref,
                 o_ref, st_ref, *, groups, chunk, cps, nlev):
    c = chunk

    @pl.when(pl.program_id(1) == 0)
    def _():
        st_ref[...] = s0_ref[...]

    lbb = jnp.clip(lb_ref[...], 0.0, 1.0)
    wmat = w_ref[...]
    for g in range(groups):
        a = f_ref[g]
        q_all = q_ref[g]
        iv16_all = i_ref[g].astype(BF)
        f = lbb + (1.0 - lbb) * _sigmoid(a)
        lf = jnp.log(jnp.maximum(f, F_FLOOR))
        kk_all = (1.0 - lbb) * _sigmoid(-a)
        lf_cat = jnp.concatenate([lf[cc * c:(cc + 1) * c] for cc in range(cps)], axis=1)
        e_cat = _mm_exact_lhs(wmat, lf_cat)
        st = st_ref[g]
        outs = []
        for cc in range(cps):
            e = e_cat[:, cc * GW:(cc + 1) * GW]
            q = q_all[cc * c:(cc + 1) * c]
            kk = kk_all[cc * c:(cc + 1) * c]
            iv16 = iv16_all[cc * c:(cc + 1) * c]
            b = e[0:c]
            qb16 = (q * jnp.exp(b)).astype(BF)
            kr16 = (kk * jnp.exp(e[c:2 * c])).astype(BF)
            dec_last = jnp.exp(b[c - 1:c])
            qt = [(q * jnp.exp(e[(2 + l) * c:(3 + l) * c])).astype(BF) for l in range(nlev)]
            kt = [(kk * jnp.exp(e[(2 + l) * c:(3 + l) * c])).astype(BF) for l in range(nlev)]
            qt.append(q.astype(BF))
            kt.append(kk.astype(BF))
            st16 = st.astype(BF)
            o_heads, st_heads = [], []
            for h in range(NH):
                sl = slice(h * HEAD_DIM, (h + 1) * HEAD_DIM)
                att = jnp.zeros((c, c), F32)
                for l in range(nlev + 1):
                    att = att + mk_ref[l] * _mm_nt(qt[l][:, sl], kt[l][:, sl])
                o_heads.append(_mm(att.astype(BF), iv16[:, sl]) + _mm_nt(qb16[:, sl], st16[:, sl]))
                st_heads.append(st[:, sl] * dec_last[:, sl] + _mm_tn(iv16[:, sl], kr16[:, sl]))
            outs.append(jnp.concatenate(o_heads, axis=1))
            st = jnp.concatenate(st_heads, axis=1)
        st_ref[g] = st
        o = jnp.concatenate(outs, axis=0)
        oh, ol = _split2(o * o)
        ms = _mm(oh, bd_ref[...]) + _mm(ol, bd_ref[...])
        o_ref[g] = o * lax.rsqrt(ms + EPS) * hg_ref[...] * _silu(g_ref[g])


def _hgrn(u3, lb, hg, s0t, chunk, groups, cps):
    bsz, t, _ = u3.shape
    nlev = int(math.log2(chunk))
    wmat, masks = _hgrn_consts(chunk)
    bd = jnp.asarray(np.kron(np.eye(NH), np.full((HEAD_DIM, HEAD_DIM), 1.0 / HEAD_DIM)), BF)
    rows = chunk * cps
    col = lambda k: pl.BlockSpec((groups, rows, GW), lambda b, c: (b, c, k))
    const = lambda shape: pl.BlockSpec(shape, lambda b, c: (0,) * len(shape))
    return pl.pallas_call(
        functools.partial(_hgrn_kernel, groups=groups, chunk=chunk, cps=cps, nlev=nlev),
        out_shape=(jax.ShapeDtypeStruct((bsz, t, GW), F32),
                   jax.ShapeDtypeStruct((bsz, HEAD_DIM, GW), F32)),
        grid=(bsz // groups, t // rows),
        in_specs=[col(0), col(1), col(2), col(3), const((1, GW)), const((1, GW)),
                  const(wmat.shape), const(masks.shape), const((GW, GW)),
                  pl.BlockSpec((groups, HEAD_DIM, GW), lambda b, c: (b, 0, 0))],
        out_specs=(pl.BlockSpec((groups, rows, GW), lambda b, c: (b, c, 0)),
                   pl.BlockSpec((groups, HEAD_DIM, GW), lambda b, c: (b, 0, 0))),
        compiler_params=_params(),
        name="hgrn2",
    )(u3, u3, u3, u3, lb, hg, wmat, masks, bd, s0t)


def _topk_rows(scores, valid, idx):
    cur = jnp.where(valid, scores, NEG_BIG)
    picked = jnp.zeros(scores.shape, F32)
    big = jnp.int32(1 << 30)
    for _ in range(MOBA_TOPK):
        mx = jnp.max(cur, axis=0, keepdims=True)
        first = jnp.min(jnp.where(cur == mx, idx, big), axis=0, keepdims=True)
        pick = idx == first
        picked = jnp.where(pick & valid, 1.0, picked)
        cur = jnp.where(pick, -jnp.inf, cur)
    return picked


def _pair_weights(qt16, tq):
    z = jnp.zeros((HEAD_DIM, tq), qt16.dtype)
    out = []
    for p in range(NH // 2):
        a = qt16[(2 * p) * HEAD_DIM:(2 * p + 1) * HEAD_DIM]
        b = qt16[(2 * p + 1) * HEAD_DIM:(2 * p + 2) * HEAD_DIM]
        out.append(jnp.concatenate([jnp.concatenate([a, z], axis=1), jnp.concatenate([z, b], axis=1)], axis=0))
    return out


def _moba_kernel(q_ref, k_ref, vt_ref, km_ref, o_ref, m_ref, l_ref, acc_ref, sel_ref, *, nb):
    own = pl.program_id(1)
    q = q_ref[...]
    tq = q.shape[0]
    qt = jnp.transpose(q)
    rowi = lax.broadcasted_iota(jnp.int32, (nb, tq), 0)
    km = km_ref[...]
    for h in range(NH):
        sl = slice(h * HEAD_DIM, (h + 1) * HEAD_DIM)
        sel_ref[h] = _topk_rows(_mm3(km[:, sl], qt[sl, :]), rowi < own, rowi)
    qp = _pair_weights((qt * SCALE).astype(BF), tq)
    m_ref[...] = jnp.full(m_ref.shape, -jnp.inf, F32)
    l_ref[...] = jnp.zeros(l_ref.shape, F32)
    acc_ref[...] = jnp.zeros(acc_ref.shape, F32)

    def step(start, nkeys, keep_fn):
        kblk = k_ref[pl.ds(start, nkeys), :]
        for p in range(NH // 2):
            sp = _mm(kblk[:, p * 2 * HEAD_DIM:(p + 1) * 2 * HEAD_DIM], qp[p])
            for r in range(2):
                h = 2 * p + r
                rows = slice(h * HEAD_DIM, (h + 1) * HEAD_DIM)
                s = jnp.where(keep_fn(h), sp[:, r * tq:(r + 1) * tq], NEG_BIG)
                m_old = m_ref[h, 0:1, :]
                m_new = jnp.maximum(m_old, jnp.max(s, axis=0, keepdims=True))
                pt = jnp.exp(s - m_new)
                alpha = jnp.exp(m_old - m_new)
                l_ref[h, 0:1, :] = alpha * l_ref[h, 0:1, :] + jnp.sum(pt, axis=0, keepdims=True)
                vt = vt_ref[0, rows, pl.ds(start, nkeys)].astype(BF)
                acc_ref[rows, :] = alpha * acc_ref[rows, :] + _mm(vt, pt.astype(BF))
                m_ref[h, 0:1, :] = m_new

    def selected(h, n):
        return jnp.broadcast_to(sel_ref[h, pl.ds(n, 1), :], (MOBA_BLOCK, tq))

    def past_pair(i, carry):
        n = 2 * i
        step(pl.multiple_of(n * MOBA_BLOCK, 2 * MOBA_BLOCK), 2 * MOBA_BLOCK,
             lambda h: jnp.concatenate([selected(h, n), selected(h, n + 1)], axis=0) > 0.5)
        return carry

    lax.fori_loop(0, own // 2, past_pair, 0)

    @pl.when(own % 2 == 1)
    def _():
        step(pl.multiple_of((own - 1) * MOBA_BLOCK, MOBA_BLOCK), MOBA_BLOCK, lambda h: selected(h, own - 1) > 0.5)

    keyi = lax.broadcasted_iota(jnp.int32, (MOBA_BLOCK, tq), 0)
    qi = lax.broadcasted_iota(jnp.int32, (MOBA_BLOCK, tq), 1)
    step(pl.multiple_of(own * MOBA_BLOCK, MOBA_BLOCK), MOBA_BLOCK, lambda h: keyi <= qi)
    out_t = jnp.concatenate(
        [acc_ref[h * HEAD_DIM:(h + 1) * HEAD_DIM, :] / l_ref[h, 0:1, :] for h in range(NH)], axis=0)
    o_ref[...] = jnp.transpose(out_t)


def _moba_prompt(qrot, k16, vt16, kmean, nbatch):
    n = qrot.shape[0]
    t = n // nbatch
    nq = t // MOBA_BLOCK
    nb = t // MOBA_BLOCK
    return pl.pallas_call(
        functools.partial(_moba_kernel, nb=nb),
        out_shape=jax.ShapeDtypeStruct((n, GW), F32),
        grid=(nbatch, nq),
        in_specs=[pl.BlockSpec((MOBA_BLOCK, GW), lambda b, j: (b * nq + j, 0)),
                  pl.BlockSpec((t, GW), lambda b, j: (b, 0)),
                  pl.BlockSpec((1, GW, t), lambda b, j: (b, 0, 0)),
                  pl.BlockSpec((nb, GW), lambda b, j: (b, 0))],
        out_specs=pl.BlockSpec((MOBA_BLOCK, GW), lambda b, j: (b * nq + j, 0)),
        scratch_shapes=[pltpu.VMEM((NH, 8, MOBA_BLOCK), F32), pltpu.VMEM((NH, 8, MOBA_BLOCK), F32),
                        pltpu.VMEM((GW, MOBA_BLOCK), F32), pltpu.VMEM((NH, nb, MOBA_BLOCK), F32)],
        compiler_params=_params(),
        name="moba_prompt",
    )(qrot, k16, vt16, kmean)


SB_BLOCK = 256
SB_KEY_STEP = 256
SB_CUTOFF = 104.0
SB_PAGES_PER_CHECK = 2


def _later_mask(n, rows_are_keys):
    r = np.arange(n)
    m = r[None, :] > r[:, None]
    return jnp.asarray((m if rows_are_keys else m.T).astype(np.float32), BF)


def _sb_kernel(q_ref, k_ref, vt_ref, up_ref, o_ref, carry_ref, acc_ref):
    j = pl.program_id(1)
    tq = q_ref.shape[0]
    qp = _pair_weights((jnp.transpose(q_ref[...]) * SCALE).astype(BF), tq)
    lower = up_ref[...]
    carry_ref[...] = jnp.zeros(carry_ref.shape, F32)
    acc_ref[...] = jnp.zeros(acc_ref.shape, F32)

    def block(start, mask):
        kblk = k_ref[pl.ds(start, SB_KEY_STEP), :]
        for p in range(NH // 2):
            zp = _mm(kblk[:, p * 2 * HEAD_DIM:(p + 1) * 2 * HEAD_DIM], qp[p])
            for r in range(2):
                h = 2 * p + r
                rows = slice(h * HEAD_DIM, (h + 1) * HEAD_DIM)
                lpos, lneg = _log_sigmoid_pair(zp[:, r * tq:(r + 1) * tq])
                if mask is not None:
                    lneg = jnp.where(mask, lneg, 0.0)
                hi, lo = _split2(lneg)
                rest = carry_ref[h:h + 1, :] + _mm(lower, hi) + _mm(lower, lo)
                a = jnp.exp(lpos + jnp.minimum(rest, 0.0))
                if mask is not None:
                    a = jnp.where(mask, a, 0.0)
                vt = vt_ref[0, rows, pl.ds(start, SB_KEY_STEP)].astype(BF)
                acc_ref[rows, :] = acc_ref[rows, :] + _mm(vt, a.astype(BF))
                carry_ref[h:h + 1, :] = carry_ref[h:h + 1, :] + jnp.sum(lneg, axis=0, keepdims=True)

    def active():
        return (jnp.max(carry_ref[0:NH, :]) > -SB_CUTOFF).astype(jnp.int32)

    keyi = lax.broadcasted_iota(jnp.int32, (SB_KEY_STEP, tq), 0)
    qi = lax.broadcasted_iota(jnp.int32, (SB_KEY_STEP, tq), 1)
    steps_per_tile = SB_BLOCK // SB_KEY_STEP
    for d in range(steps_per_tile - 1, -1, -1):
        block(pl.multiple_of(j * SB_BLOCK + d * SB_KEY_STEP, SB_KEY_STEP), keyi + d * SB_KEY_STEP < qi)

    def body(c):
        block(pl.multiple_of(c[0] * SB_KEY_STEP, SB_KEY_STEP), None)
        return c[0] - 1, active()

    lax.while_loop(lambda c: (c[0] >= 0) & (c[1] > 0), body, (j * steps_per_tile - 1, active()))
    o_ref[...] = jnp.transpose(acc_ref[...])


def _sb_prompt(u, u16, vt16, nbatch):
    n = u.shape[0]
    t = n // nbatch
    nq = t // SB_BLOCK
    return pl.pallas_call(
        _sb_kernel,
        out_shape=jax.ShapeDtypeStruct((n, GW), F32),
        grid=(nbatch, nq),
        in_specs=[pl.BlockSpec((SB_BLOCK, GW), lambda b, j: (b * nq + j, 7)),
                  pl.BlockSpec((t, GW), lambda b, j: (b, 2)),
                  pl.BlockSpec((1, GW, t), lambda b, j: (b, 3, 0)),
                  pl.BlockSpec((SB_KEY_STEP, SB_KEY_STEP), lambda b, j: (0, 0))],
        out_specs=pl.BlockSpec((SB_BLOCK, GW), lambda b, j: (b * nq + j, 0)),
        scratch_shapes=[pltpu.VMEM((8, SB_BLOCK), F32), pltpu.VMEM((GW, SB_BLOCK), F32)],
        compiler_params=_params(),
        name="sb_prompt",
    )(u, u16, vt16, _later_mask(SB_KEY_STEP, True))


def _block_diag_q(q, nbatch, t):
    q4 = q.reshape(nbatch, t, NH, HEAD_DIM)
    eye = jnp.eye(NH, dtype=q.dtype)
    bd = q4[:, None, :, :, :] * eye[None, :, None, :, None]
    return bd.reshape(nbatch, NH * t, GW)


def _page_specs(layer, n_phys, n_pages):
    def spec(p):
        return pl.BlockSpec((1, GW, PAGE_SIZE), lambda b, pt: (layer * n_phys + pt[b * n_pages + p], 0, 0))
    return [spec(p) for p in range(n_pages)]


def _heads_out(acc, den, t):
    return jnp.concatenate(
        [acc[h * t:(h + 1) * t, h * HEAD_DIM:(h + 1) * HEAD_DIM] / den[h * t:(h + 1) * t]
         for h in range(NH)], axis=1)


def _pad_rows(x, rows):
    return jnp.concatenate([x, jnp.zeros((rows - x.shape[0], x.shape[1]), x.dtype)], axis=0)


def _moba_dec_kernel(pt_ref, qbd_ref, knew_ref, vnew_ref, *refs, n_pages, t, own):
    del pt_ref
    kt_refs, vt_refs, o_ref = refs[:n_pages], refs[n_pages:2 * n_pages], refs[2 * n_pages]
    qbd = qbd_ref[0]
    nrow = qbd.shape[0]
    ppb = MOBA_BLOCK // PAGE_SIZE
    lg = [_mm3(qbd, kt_refs[p][0]) for p in range(n_pages)]
    score = []
    for n in range(own):
        tot = jnp.sum(lg[n * ppb], axis=1, keepdims=True)
        for r in range(1, ppb):
            tot = tot + jnp.sum(lg[n * ppb + r], axis=1, keepdims=True)
        score.append(tot * (1.0 / MOBA_BLOCK))
    sel = []
    for n in range(own):
        rank = jnp.zeros((nrow, 1), F32)
        for m in range(own):
            if m != n:
                beats = (score[m] >= score[n]) if m < n else (score[m] > score[n])
                rank = rank + jnp.where(beats, 1.0, 0.0)
        sel.append(rank < float(MOBA_TOPK))
    logits = [jnp.where(sel[p // ppb], lg[p] * SCALE, NEG_BIG) for p in range(n_pages)]
    lnew = _mm3_nt(qbd, _pad_rows(knew_ref[...], PAGE_SIZE)) * SCALE
    keyi = lax.broadcasted_iota(jnp.int32, lnew.shape, 1)
    qi = lax.broadcasted_iota(jnp.int32, lnew.shape, 0) % t
    lnew = jnp.where(keyi <= qi, lnew, NEG_BIG)
    m = jnp.max(lnew, axis=1, keepdims=True)
    for lp in logits:
        m = jnp.maximum(m, jnp.max(lp, axis=1, keepdims=True))
    enew = jnp.exp(lnew - m)
    den = jnp.sum(enew, axis=1, keepdims=True)
    acc = _mm(enew.astype(BF), _pad_rows(vnew_ref[...], PAGE_SIZE).astype(BF))
    for p in range(n_pages):
        e = jnp.exp(logits[p] - m)
        den = den + jnp.sum(e, axis=1, keepdims=True)
        acc = acc + _mm_nt(e.astype(BF), vt_refs[p][0].astype(BF))
    o_ref[...] = _heads_out(acc, den, t)


def _sb_dec_kernel(pt_ref, qbd_ref, knew_ref, vnew_ref, later_ref, *refs, n_pages, t):
    del pt_ref
    kt_refs, vt_refs, o_ref = refs[:n_pages], refs[n_pages:2 * n_pages], refs[2 * n_pages]
    qbd16 = (qbd_ref[0] * SCALE).astype(BF)
    nrow = qbd16.shape[0]
    later = later_ref[...]

    def block(z, carry, mask):
        lpos, lneg = _log_sigmoid_pair(z)
        if mask is not None:
            lneg = jnp.where(mask, lneg, 0.0)
        hi, lo = _split2(lneg)
        rest = carry + _mm(hi, later) + _mm(lo, later)
        a = jnp.exp(lpos + jnp.minimum(rest, 0.0))
        if mask is not None:
            a = jnp.where(mask, a, 0.0)
        return a.astype(BF), carry + jnp.sum(lneg, axis=1, keepdims=True)

    keyi = lax.broadcasted_iota(jnp.int32, (nrow, PAGE_SIZE), 1)
    qi = lax.broadcasted_iota(jnp.int32, (nrow, PAGE_SIZE), 0) % t
    znew = _mm_nt(qbd16, _pad_rows(knew_ref[...], PAGE_SIZE).astype(BF))
    a, carry = block(znew, jnp.zeros((nrow, 1), F32), keyi < qi)
    acc = _mm(a, _pad_rows(vnew_ref[...], PAGE_SIZE).astype(BF))
    for p0 in range(n_pages - 1, -1, -SB_PAGES_PER_CHECK):
        def visit(state, p0=p0):
            acc_v, carry_v = state
            for p in range(p0, max(p0 - SB_PAGES_PER_CHECK, -1), -1):
                a, carry_v = block(_mm(qbd16, kt_refs[p][0].astype(BF)), carry_v, None)
                acc_v = acc_v + _mm_nt(a, vt_refs[p][0].astype(BF))
            return acc_v, carry_v
        acc, carry = lax.cond(jnp.max(carry) > -SB_CUTOFF, visit, lambda state: state, (acc, carry))
    o_ref[...] = _heads_out(acc, jnp.ones((nrow, 1), F32), t)


def _dec_attention(kind, layer, page_table, qbd, knew, vnew, kcache, vcache, t, pos0):
    nbatch, n_pages = page_table.shape
    n_phys = kcache.shape[0] // DEPTH
    assert pos0 == n_pages * PAGE_SIZE and pos0 % MOBA_BLOCK == 0 and t <= PAGE_SIZE
    pspecs = _page_specs(layer, n_phys, n_pages)
    row = pl.BlockSpec((t, GW), lambda b, pt: (b, 0))
    in_specs = [pl.BlockSpec((1, NH * t, GW), lambda b, pt: (b, 0, 0)), row, row]
    args = [qbd, knew, vnew]
    if kind == "moba":
        body = functools.partial(_moba_dec_kernel, n_pages=n_pages, t=t, own=pos0 // MOBA_BLOCK)
    else:
        body = functools.partial(_sb_dec_kernel, n_pages=n_pages, t=t)
        in_specs.append(pl.BlockSpec((PAGE_SIZE, PAGE_SIZE), lambda b, pt: (0, 0)))
        args.append(_later_mask(PAGE_SIZE, False))
    in_specs += pspecs + pspecs
    args += [kcache] * n_pages + [vcache] * n_pages
    return pl.pallas_call(
        body,
        out_shape=jax.ShapeDtypeStruct((nbatch * t, GW), F32),
        grid_spec=pltpu.PrefetchScalarGridSpec(
            num_scalar_prefetch=1, grid=(nbatch,), in_specs=in_specs,
            out_specs=pl.BlockSpec((t, GW), lambda b, pt: (b, 0))),
        compiler_params=_params(),
        name=kind + "_decode",
    )(page_table.reshape(-1), *args)


POOL_HALO = 16


def _pool_bands(seqs, t):
    halo_seq = np.repeat(np.arange(seqs), POOL_HALO)
    halo_e = np.tile(np.arange(POOL_HALO), seqs)
    row_seq = np.repeat(np.arange(seqs), t)
    row_e = np.tile(np.arange(t), seqs) + POOL_HALO
    col_seq = np.concatenate([halo_seq, row_seq])[None, :]
    col_e = np.concatenate([halo_e, row_e])[None, :]
    same = col_seq == row_seq[:, None]
    r = row_e[:, None]
    return jnp.asarray(np.stack([(same & (col_e <= r) & (col_e > r - w)).astype(np.float32)
                                 for w in POOL_WINDOWS]), BF)


def _pool_kernel(u_ref, halo_ref, buf_ref, band_ref, pw_ref, ps_ref, o_ref, *, rows, nt, t_seq, pos0):
    i = pl.program_id(0)
    u = u_ref[...]
    halo = jnp.where(i % nt == 0, buf_ref[...], halo_ref[...])
    ext = jnp.concatenate([halo, u], axis=0)
    hi, mid, lo = _split3(ext)
    tok = (i * rows + lax.broadcasted_iota(jnp.int32, (rows, 1), 0)) % t_seq
    pos = (pos0 + tok).astype(F32)
    parts = []
    gc = GW // len(POOL_WINDOWS)
    for gi, w in enumerate(POOL_WINDOWS):
        sl = slice(gi * gc, (gi + 1) * gc)
        band = band_ref[gi]
        ssum = _mm(band, hi[:, sl]) + _mm(band, mid[:, sl]) + _mm(band, lo[:, sl])
        cnt = jnp.minimum(float(w), pos + 1.0)
        parts.append(ssum / cnt - u[:, sl])
    d = jnp.concatenate(parts, axis=1)
    o_ref[...] = _mm(d.astype(BF), pw_ref[...]) * ps_ref[...]


def _pool(u, buf_flat, pw_bd16, ps, nbatch, t, pos0):
    n = u.shape[0]
    if t >= MOBA_BLOCK:
        rows, seqs, hr = 256, 1, POOL_HALO
        nt = t // rows
        band = _pool_bands(1, rows)
        halo_arr = u
        halo_spec = pl.BlockSpec((hr, GW), lambda i: (jnp.maximum(i * (rows // hr) - 1, 0), 10))
        buf_spec = pl.BlockSpec((hr, GW), lambda i: (i // nt, 0))
    else:
        seqs = 128 // t
        rows, hr, nt = seqs * t, seqs * POOL_HALO, 1
        band = _pool_bands(seqs, t)
        halo_arr = buf_flat
        halo_spec = pl.BlockSpec((hr, GW), lambda i: (i, 0))
        buf_spec = halo_spec
    return pl.pallas_call(
        functools.partial(_pool_kernel, rows=rows, nt=nt, t_seq=t, pos0=pos0),
        out_shape=jax.ShapeDtypeStruct((n, GW), F32),
        grid=(n // rows,),
        in_specs=[pl.BlockSpec((rows, GW), lambda i: (i, 10)), halo_spec, buf_spec,
                  pl.BlockSpec(band.shape, lambda i: (0, 0, 0)),
                  pl.BlockSpec((GW, GW), lambda i: (0, 0)),
                  pl.BlockSpec((1, GW), lambda i: (0, 0))],
        out_specs=pl.BlockSpec((rows, GW), lambda i: (i, 0)),
        compiler_params=_params(),
        name="pool_mixer",
    )(u, halo_arr, buf_flat, band, pw_bd16, ps)


def _route(logits):
    lane = lax.broadcasted_iota(jnp.int32, logits.shape, 1)
    big = jnp.int32(1 << 30)
    gmask = lane < N_GROUPS
    gl = jnp.where(gmask, logits, -jnp.inf)
    gmax = jnp.max(gl, axis=-1, keepdims=True)
    gsum = jnp.sum(jnp.where(gmask, jnp.exp(gl - gmax), 0.0), axis=-1, keepdims=True)
    g_top = 1.0 / gsum
    g_idx = jnp.min(jnp.where(gl == gmax, lane, big), axis=-1, keepdims=True)
    emask = (lane >= N_GROUPS) & (lane < N_GROUPS + N_EXPERTS) & ((lane - N_GROUPS) // EXP_PER_GROUP == g_idx)
    el = jnp.where(emask, logits, -jnp.inf)
    m1 = jnp.max(el, axis=-1, keepdims=True)
    i1 = jnp.min(jnp.where(el == m1, lane, big), axis=-1, keepdims=True)
    el2 = jnp.where(lane == i1, -jnp.inf, el)
    m2 = jnp.max(el2, axis=-1, keepdims=True)
    i2 = jnp.min(jnp.where(el2 == m2, lane, big), axis=-1, keepdims=True)
    r = jnp.exp(m2 - m1)
    w1 = g_top / (1.0 + r)
    w2 = g_top * r / (1.0 + r)
    gate = jnp.where(lane == i1, w1, jnp.where(lane == i2, w2, 0.0))
    info = jnp.where(lane == 0, (i1 - N_GROUPS).astype(F32),
                     jnp.where(lane == 1, (i2 - N_GROUPS).astype(F32),
                               jnp.where(lane == 2, w1, jnp.where(lane == 3, w2, 0.0))))
    return gate, info


def _outproj_kernel(oa_ref, ob_ref, oc_ref, od_ref, x_ref, gt_ref, g_ref, sc_ref, sh_ref, w_ref,
                    wr_ref, br_ref, x1_ref, h_ref, gate_ref, info_ref):
    merged = jnp.concatenate([oa_ref[...], ob_ref[...], oc_ref[...], od_ref[...]], axis=1).astype(BF)
    x1 = x_ref[...] + _mod_val(gt_ref) * _mm(merged, w_ref[...])
    x1_ref[...] = x1
    h = _rms_mod(x1, g_ref[...], _mod_val(sc_ref), _mod_val(sh_ref))
    h_ref[...] = h.astype(h_ref.dtype)
    gate_ref[...], info_ref[...] = _route(_mm3(h, wr_ref[...]) + br_ref[...])


def _outproj(oa, ob, oc, od, x, gt, g, sc, sh, w16, wr, br, rows_per_group, tm, h_dtype):
    n = x.shape[0]
    quarter = pl.BlockSpec((tm, GW), lambda i: (i, 0))
    full = pl.BlockSpec((tm, D_MODEL), lambda i: (i, 0))
    narrow = pl.BlockSpec((tm, LANES), lambda i: (i, 0))
    return pl.pallas_call(
        _outproj_kernel,
        out_shape=(jax.ShapeDtypeStruct((n, D_MODEL), F32), jax.ShapeDtypeStruct((n, D_MODEL), h_dtype),
                   jax.ShapeDtypeStruct((n, LANES), F32), jax.ShapeDtypeStruct((n, LANES), F32)),
        grid=(n // tm,),
        in_specs=[quarter, quarter, quarter, quarter, full,
                  _mod_spec(gt, tm, rows_per_group),
                  pl.BlockSpec((1, D_MODEL), lambda i: (0, 0)),
                  _mod_spec(sc, tm, rows_per_group), _mod_spec(sh, tm, rows_per_group),
                  pl.BlockSpec((D_MODEL, D_MODEL), lambda i: (0, 0)),
                  pl.BlockSpec((D_MODEL, LANES), lambda i: (0, 0)),
                  pl.BlockSpec((1, LANES), lambda i: (0, 0))],
        out_specs=(full, full, narrow, narrow),
        compiler_params=_params(),
        name="out_proj_router",
    )(oa, ob, oc, od, x, gt.arr, g, sc.arr, sh.arr, w16, wr, br)


def _moe_kernel(h_ref, gate_ref, x1_ref, gt_ref, wg_ref, wu_ref, wd_ref, o_ref, acc_ref):
    e = pl.program_id(1)

    @pl.when(e == 0)
    def _():
        acc_ref[...] = jnp.zeros(acc_ref.shape, F32)

    h = h_ref[...]
    a = _silu(_mm(h, wg_ref[0].astype(BF))) * _mm(h, wu_ref[0].astype(BF))
    lane = lax.broadcasted_iota(jnp.int32, gate_ref.shape, 1)
    gcol = jnp.sum(jnp.where(lane == e + N_GROUPS, gate_ref[...], 0.0), axis=-1, keepdims=True)
    acc_ref[...] += gcol * _mm(a.astype(BF), wd_ref[0].astype(BF))

    @pl.when(e == N_EXPERTS - 1)
    def _():
        o_ref[...] = x1_ref[...] + _mod_val(gt_ref) * acc_ref[...]


def _moe(h16, gate, x1, gt, wg16, wu16, wd16, layer, rows_per_group, tm):
    n = x1.shape[0]
    full = lambda: pl.BlockSpec((tm, D_MODEL), lambda i, e: (i, 0))
    gt_spec = _mod_spec(gt, tm, rows_per_group)
    return pl.pallas_call(
        _moe_kernel,
        out_shape=jax.ShapeDtypeStruct((n, D_MODEL), F32),
        grid=(n // tm, N_EXPERTS),
        in_specs=[full(), pl.BlockSpec((tm, LANES), lambda i, e: (i, 0)), full(), gt_spec,
                  pl.BlockSpec((1, D_MODEL, D_EXPERT), lambda i, e: (layer * N_EXPERTS + e, 0, 0)),
                  pl.BlockSpec((1, D_MODEL, D_EXPERT), lambda i, e: (layer * N_EXPERTS + e, 0, 0)),
                  pl.BlockSpec((1, D_EXPERT, D_MODEL), lambda i, e: (layer * N_EXPERTS + e, 0, 0))],
        out_specs=full(),
        scratch_shapes=[pltpu.VMEM((tm, D_MODEL), F32)],
        compiler_params=_params(),
        name="moe_experts",
    )(h16, gate, x1, gt.arr, wg16, wu16, wd16)


MOE_TILE = 256
MOE_COMBINE_TILE = 256
DMA_UNROLL = 8


def _moe_plan(info, tile):
    n = info.shape[0]
    e = info[:, 0:2].astype(jnp.int32).reshape(-1)
    onehot = (e[:, None] == jnp.arange(N_EXPERTS, dtype=jnp.int32)[None, :]).astype(jnp.int32)
    csum = jnp.cumsum(onehot, axis=0)
    counts = csum[-1]
    rank = jnp.sum((csum - onehot) * onehot, axis=1)
    padded = ((counts + tile - 1) // tile) * tile
    ends = jnp.cumsum(padded)
    pos = jnp.sum(onehot * (ends - padded)[None, :], axis=1) + rank
    total = 2 * n + N_EXPERTS * tile
    src = jnp.zeros((total,), jnp.int32).at[pos].set(jnp.arange(2 * n, dtype=jnp.int32) // 2)
    tstart = jnp.arange(total // tile, dtype=jnp.int32) * tile
    texp = jnp.minimum(jnp.sum((tstart[:, None] >= ends[None, :]).astype(jnp.int32), axis=1), N_EXPERTS - 1)
    used = (ends[-1] // tile).astype(jnp.int32).reshape(1)
    return src, texp.astype(jnp.int32), used, pos.astype(jnp.int32)


def _gather_rows(idx_ref, base, count, src_hbm, dst, sem, unroll, row_of):
    def body(it, carry):
        for k in range(unroll):
            aligned, off = row_of(it, k)
            row = pl.multiple_of(aligned, 8) + off
            pltpu.make_async_copy(src_hbm.at[pl.ds(idx_ref[base + it * unroll + k], 1)],
                                  dst.at[pl.ds(row, 1)], sem).start()
        return carry
    lax.fori_loop(0, count // unroll, body, 0)


def _moe_group_kernel(src_ref, te_ref, used_ref, h_hbm, wg_ref, wu_ref, wd_ref, y_ref, xbuf, sem, *, tile):
    del te_ref
    i = pl.program_id(0)
    slot = i % 2
    used = used_ref[0]

    def start(t, s):
        _gather_rows(src_ref, t * tile, tile, h_hbm, xbuf.at[s], sem.at[s], DMA_UNROLL,
                     lambda it, k: (it * DMA_UNROLL, k))

    @pl.when(i == 0)
    def _():
        start(0, 0)

    @pl.when(i + 1 < used)
    def _():
        start(i + 1, 1 - slot)

    @pl.when(i < used)
    def _():
        pltpu.make_async_copy(xbuf.at[slot], xbuf.at[slot], sem.at[slot]).wait()
        x = xbuf[slot].astype(BF)
        a = _silu(_mm(x, wg_ref[0].astype(BF))) * _mm(x, wu_ref[0].astype(BF))
        y_ref[...] = _mm(a.astype(BF), wd_ref[0].astype(BF))

    @pl.when(i >= used)
    def _():
        y_ref[...] = jnp.zeros(y_ref.shape, F32)


def _moe_combine_kernel(pos_ref, y_hbm, x1_ref, gt_ref, info_ref, o_ref, ybuf, sem, *, tc, nsteps):
    i = pl.program_id(0)
    slot = i % 2

    def start(t, s):
        _gather_rows(pos_ref, t * 2 * tc, 2 * tc, y_hbm, ybuf.at[s], sem.at[s], 2 * DMA_UNROLL,
                     lambda it, k: (it * DMA_UNROLL, (k % 2) * tc + k // 2))

    @pl.when(i == 0)
    def _():
        start(0, 0)

    @pl.when(i + 1 < nsteps)
    def _():
        start(i + 1, 1 - slot)

    pltpu.make_async_copy(ybuf.at[slot], ybuf.at[slot], sem.at[slot]).wait()
    info = info_ref[...]
    y = info[:, 2:3] * ybuf[slot, 0:tc, :] + info[:, 3:4] * ybuf[slot, tc:2 * tc, :]
    o_ref[...] = x1_ref[...] + _mod_val(gt_ref) * y


def _moe_sparse(h, info, x1, gt, wg16, wu16, wd16, layer, rows_per_group):
    n = x1.shape[0]
    tile, tc = MOE_TILE, MOE_COMBINE_TILE
    src, texp, used, pos = _moe_plan(info, tile)
    total = src.shape[0]
    wspec = lambda shape: pl.BlockSpec((1,) + shape, lambda i, s, te, u: (layer * N_EXPERTS + te[i], 0, 0))
    ys = pl.pallas_call(
        functools.partial(_moe_group_kernel, tile=tile),
        out_shape=jax.ShapeDtypeStruct((total, D_MODEL), F32),
        grid_spec=pltpu.PrefetchScalarGridSpec(
            num_scalar_prefetch=3, grid=(total // tile,),
            in_specs=[pl.BlockSpec(memory_space=pl.ANY), wspec((D_MODEL, D_EXPERT)), wspec((D_MODEL, D_EXPERT)),
                      wspec((D_EXPERT, D_MODEL))],
            out_specs=pl.BlockSpec((tile, D_MODEL), lambda i, s, te, u: (i, 0)),
            scratch_shapes=[pltpu.VMEM((2, tile, D_MODEL), F32), pltpu.SemaphoreType.DMA((2,))]),
        compiler_params=_params(),
        name="moe_grouped",
    )(src, texp, used, h, wg16, wu16, wd16)
    nsteps = n // tc
    tiles = rows_per_group // tc
    return pl.pallas_call(
        functools.partial(_moe_combine_kernel, tc=tc, nsteps=nsteps),
        out_shape=jax.ShapeDtypeStruct((n, D_MODEL), F32),
        grid_spec=pltpu.PrefetchScalarGridSpec(
            num_scalar_prefetch=1, grid=(nsteps,),
            in_specs=[pl.BlockSpec(memory_space=pl.ANY),
                      pl.BlockSpec((tc, D_MODEL), lambda i, p: (i, 0)),
                      _mod_spec(gt, tc, rows_per_group),
                      pl.BlockSpec((tc, LANES), lambda i, p: (i, 0))],
            out_specs=pl.BlockSpec((tc, D_MODEL), lambda i, p: (i, 0)),
            scratch_shapes=[pltpu.VMEM((2, 2 * tc, D_MODEL), F32), pltpu.SemaphoreType.DMA((2,))]),
        compiler_params=_params(),
        name="moe_combine",
    )(pos, ys, x1, gt.arr, info)


def _final_kernel(x_ref, g_ref, o_ref):
    x = x_ref[...]
    ms = jnp.mean(x * x, axis=-1, keepdims=True)
    o_ref[...] = x * lax.rsqrt(ms + EPS) * g_ref[...]


def _final_norm(x, g, tm):
    n = x.shape[0]
    return pl.pallas_call(
        _final_kernel,
        out_shape=jax.ShapeDtypeStruct((n, D_MODEL), F32),
        grid=(n // tm,),
        in_specs=[pl.BlockSpec((tm, D_MODEL), lambda i: (i, 0)), pl.BlockSpec((1, D_MODEL), lambda i: (0, 0))],
        out_specs=pl.BlockSpec((tm, D_MODEL), lambda i: (i, 0)),
        compiler_params=_params(),
        name="final_norm",
    )(x, g)


def _state_in(s):
    b = s.shape[0]
    return s.transpose(0, 3, 1, 2).reshape(b, HEAD_DIM, GW)


def _state_out(s):
    b = s.shape[0]
    return s.reshape(b, HEAD_DIM, NH, HEAD_DIM).transpose(0, 2, 3, 1)


def _hgrn_lower_bounds(lb_logits):
    p = jax.nn.softmax(lb_logits.astype(F32), axis=0)
    return jnp.cumsum(p, axis=0) - p[0:1]


def _layer(x, mods, wl, layer, nbatch, t, pos0, s0, pbuf, past, tm):
    sh1, sc1, gt1, sh2, sc2, gt2 = mods
    n = nbatch * t
    prompt = past is None
    cos, sin = _rope_tables((np.arange(t) if prompt else np.tile(np.arange(t), nbatch)) + pos0)
    proj = _inproj(x, wl["g_mix"], sc1, sh1, wl["w_in"], cos, sin, t, tm, prompt)
    u, u16, qrot, krot16, krot = proj[:5]
    u3 = u.reshape(nbatch, t, IN_COLS)

    chunk = HG_CHUNK if t % HG_CHUNK == 0 else t
    groups = 2 if t % HG_CHUNK == 0 else 8
    cps = HG_CHUNKS_PER_STEP if t % (HG_CHUNK * HG_CHUNKS_PER_STEP) == 0 else 1
    oa, st = _hgrn(u3, wl["lb"], wl["hg_g"], _state_in(s0), chunk, groups, cps)

    if prompt:
        krot_t, ut, kmean = krot, proj[5], proj[6]
        ob = _moba_prompt(qrot, krot16, ut, kmean.reshape(-1, GW), nbatch)
        oc = _sb_prompt(u, u16, ut, nbatch)
    else:
        page_table, cmk, cmv, csk, csv = past
        vb = u[:, 6 * GW:7 * GW]
        ob = _dec_attention("moba", layer, page_table, _block_diag_q(qrot, nbatch, t), krot, vb, cmk, cmv, t, pos0)
        qc, kc, vc = u[:, 7 * GW:8 * GW], u[:, 8 * GW:9 * GW], u[:, 9 * GW:10 * GW]
        oc = _dec_attention("sb", layer, page_table, _block_diag_q(qc, nbatch, t), kc, vc, csk, csv, t, pos0)

    buf_flat = jnp.pad(pbuf, ((0, 0), (POOL_HALO - POOL_BUF, 0), (0, 0))).reshape(nbatch * POOL_HALO, GW)
    od = _pool(u, buf_flat, wl["pool_w"], wl["pool_scale"], nbatch, t, pos0)
    if t >= POOL_BUF:
        buf_new = u3[:, t - POOL_BUF:, 10 * GW:]
    else:
        buf_new = jnp.concatenate([pbuf, u3[:, :, 10 * GW:]], axis=1)[:, -POOL_BUF:]

    grouped = past is None and t % MOE_COMBINE_TILE == 0
    x1, h, gate, info = _outproj(oa.reshape(n, GW), ob, oc, od, x, gt1, wl["g_ffn"], sc2, sh2,
                                 wl["w_out"], wl["wr"], wl["br"], t, tm, F32 if grouped else BF)
    if grouped:
        x2 = _moe_sparse(h, info, x1, gt2, wl["ewg"], wl["ewu"], wl["ewd"], layer, t)
    else:
        x2 = _moe(h, gate, x1, gt2, wl["ewg"], wl["ewu"], wl["ewd"], layer, t, tm)

    if prompt:
        heads = lambda zt: zt.reshape(nbatch, NH, HEAD_DIM, t).transpose(0, 3, 1, 2)
        kv = (heads(krot_t), heads(ut[:, 0:GW]), heads(ut[:, 2 * GW:3 * GW]), heads(ut[:, 3 * GW:4 * GW]))
    else:
        rows = lambda z: z.reshape(nbatch, t, NH, HEAD_DIM)
        kv = (rows(krot), rows(u[:, 6 * GW:7 * GW]), rows(u[:, 8 * GW:9 * GW]), rows(u[:, 9 * GW:10 * GW]))
    return x2, kv + (_state_out(st), buf_new)


def kernel(x_prompt, x_sample, cache_moba_k, cache_moba_v, cache_sb_k, cache_sb_v, state_hgrn, state_pool,
           page_table, c_prompt, c_sample, ada_w, ada_b, norm_mix_g, norm_ffn_g, w_in, hgrn_lb, hgrn_norm_g,
           pool_w, pool_scale, w_out, router_group_w, router_group_b, router_expert_w, router_expert_b,
           expert_w_gate, expert_w_up, expert_w_down, final_norm_g):
    bp, tp, _ = x_prompt.shape
    bs, ts, _ = x_sample.shape
    pos0_s = page_table.shape[1] * PAGE_SIZE
    n_phys = cache_moba_k.shape[1]

    c_all = jnp.concatenate([c_prompt, c_sample], axis=0)
    mpad = (-c_all.shape[0]) % 8
    mod = _ada_all(jnp.pad(c_all, ((0, mpad), (0, 0))), ada_w, ada_b)

    mod_p = mod[:, :bp].reshape(DEPTH * bp * 6, 1, D_MODEL)
    mod_s = jnp.repeat(mod[:, bp:bp + bs], ts, axis=1)

    lbs = _hgrn_lower_bounds(hgrn_lb)
    w_in16 = w_in.astype(BF)
    w_out16 = w_out.astype(BF)
    ewg_all = expert_w_gate.reshape(DEPTH * N_EXPERTS, D_MODEL, D_EXPERT)
    ewu_all = expert_w_up.reshape(DEPTH * N_EXPERTS, D_MODEL, D_EXPERT)
    ewd_all = expert_w_down.reshape(DEPTH * N_EXPERTS, D_EXPERT, D_MODEL)
    wr = jnp.concatenate([router_group_w, router_expert_w], axis=-1)
    wr = jnp.pad(wr, ((0, 0), (0, 0), (0, LANES - wr.shape[-1])))
    br = jnp.concatenate([router_group_b, router_expert_b], axis=-1)
    br = jnp.pad(br, ((0, 0), (0, LANES - br.shape[-1])))
    caches = tuple(c.transpose(0, 1, 3, 4, 2).reshape(DEPTH * n_phys, GW, PAGE_SIZE)
                   for c in (cache_moba_k, cache_moba_v, cache_sb_k, cache_sb_v))

    xp = x_prompt.reshape(bp * tp, D_MODEL)
    xs = x_sample.reshape(bs * ts, D_MODEL)
    st_p, st_s = [], []
    for l in range(DEPTH):
        pw_bd = jnp.zeros((GW, GW), F32)
        gc = GW // len(POOL_WINDOWS)
        for gi in range(len(POOL_WINDOWS)):
            pw_bd = lax.dynamic_update_slice(pw_bd, pool_w[l, gi], (gi * gc, gi * gc))
        wl = dict(g_mix=norm_mix_g[l][None], g_ffn=norm_ffn_g[l][None], w_in=w_in16[l], lb=lbs[l][None],
                  hg_g=hgrn_norm_g[l][None], pool_w=pw_bd.astype(BF), pool_scale=pool_scale[l][None],
                  w_out=w_out16[l], wr=wr[l], br=br[l][None], ewg=ewg_all, ewu=ewu_all, ewd=ewd_all)
        mods_p = tuple(_Mod(mod_p, True, l, k, bp) for k in range(6))
        mods_s = tuple(_Mod(mod_s, False, l, k, bs) for k in range(6))
        xp, sp = _layer(xp, mods_p, wl, l, bp, tp, 0, jnp.zeros((bp, NH, HEAD_DIM, HEAD_DIM), F32),
                        jnp.zeros((bp, POOL_BUF, GW), F32), None, min(ROW_TILE, tp))
        xs, ss = _layer(xs, mods_s, wl, l, bs, ts, pos0_s, state_hgrn[l], state_pool[l],
                        (page_table,) + caches, min(ROW_TILE, bs * ts))
        st_p.append(sp)
        st_s.append(ss)

    y_prompt = _final_norm(xp, final_norm_g[None], min(ROW_TILE, tp)).reshape(bp, tp, D_MODEL)
    y_sample = _final_norm(xs, final_norm_g[None], min(ROW_TILE, bs * ts)).reshape(bs, ts, D_MODEL)
    stk = lambda sts, i: jnp.stack([s[i] for s in sts], axis=0)
    return (y_prompt, y_sample,
            stk(st_p, 0), stk(st_p, 1), stk(st_p, 2), stk(st_p, 3), stk(st_p, 4), stk(st_p, 5),
            stk(st_s, 0), stk(st_s, 1), stk(st_s, 2), stk(st_s, 3), stk(st_s, 4), stk(st_s, 5))
```
